```python
import math
import jax, jax.numpy as jnp
from jax import lax
import numpy as np

D_MODEL = 2048
BATCH = 4
SEQ = 2048
DEPTH = 2

HEAD_DIM = 128
ROT_DIM = HEAD_DIM // 4
ROPE_THETA = 500000.0
A_PATTERNS = ((128, 1), (512, 4), (2048, 16))
A_GROUPS = len(A_PATTERNS)
A_HEADS = 16
B_HEADS = 16
MOBA_BLOCK = 256
MOBA_TOPK = 3
MOBA_QCHUNK = 16
N_EXPERTS = 32
TOP_K = 4
D_FF = 2048
SWIGLU_LIMIT = 7.0
SWIGLU_ALPHA = 1.702
MOE_BLOCK = 128
N_A_LAYERS = DEPTH // 2
N_B_LAYERS = DEPTH - N_A_LAYERS
DEEPNORM_ALPHA = (2 * DEPTH) ** 0.25
DEEPNORM_BETA = (8 * DEPTH) ** -0.25
LN_EPS = 1e-5
NEG_INF = -1e30

kernel_name = 'yoco_dilated_moba_moe_deepnorm'


def layer_norm(x, g, b):
    xf = x.astype(jnp.float32)
    mu = jnp.mean(xf, axis=-1, keepdims=True)
    var = jnp.mean(jnp.square(xf - mu), axis=-1, keepdims=True)
    return ((xf - mu) * lax.rsqrt(var + LN_EPS) * g + b).astype(x.dtype)


def rope_tables(seq):
    inv = ROPE_THETA ** (-jnp.arange(0, ROT_DIM, 2, dtype=jnp.float32) / ROT_DIM)
    ang = jnp.arange(seq, dtype=jnp.float32)[:, None] * inv[None, :]
    return jnp.cos(ang), jnp.sin(ang)


def apply_partial_rope(x, cos, sin):
    half = ROT_DIM // 2
    c = cos[None, :, None, :]
    s = sin[None, :, None, :]
    x1 = x[..., :half].astype(jnp.float32)
    x2 = x[..., half:ROT_DIM].astype(jnp.float32)
    rot = jnp.concatenate([x1 * c - x2 * s, x2 * c + x1 * s], axis=-1).astype(x.dtype)
    return jnp.concatenate([rot, x[..., ROT_DIM:]], axis=-1)


def banded_causal_attention(q, k, v, span):
    n, L, h, dh = q.shape
    blk = span
    pad = (-L) % blk
    nb = (L + pad) // blk
    qb = jnp.pad(q, ((0, 0), (0, pad), (0, 0), (0, 0))).reshape(n, nb, blk, h, dh)

    def key_blocks(t):
        tp = jnp.pad(t, ((0, 0), (blk, pad), (0, 0), (0, 0))).reshape(n, nb + 1, blk, h, dh)
        return jnp.concatenate([tp[:, :-1], tp[:, 1:]], axis=2)

    kb = key_blocks(k)
    vb = key_blocks(v)
    s = jnp.einsum('nbqhd,nbkhd->nbhqk', qb, kb).astype(jnp.float32) * (dh ** -0.5)
    qi = jnp.arange(blk)[:, None]
    ki = jnp.arange(2 * blk)[None, :] - blk
    diff = qi - ki
    band = (diff >= 0) & (diff <= span)
    not_before_start = (jnp.arange(nb)[:, None, None] > 0) | (ki[None] >= 0)
    mask = band[None] & not_before_start
    s = jnp.where(mask[None, :, None], s, NEG_INF)
    m = jnp.max(s, axis=-1, keepdims=True)
    p = jnp.exp(s - m)
    den = jnp.sum(p, axis=-1)
    o = jnp.einsum('nbhqk,nbkhd->nbqhd', p, vb.astype(jnp.float32))
    o = o / jnp.moveaxis(den, -1, 2)[..., None]
    lse = jnp.moveaxis(m[..., 0] + jnp.log(den), -1, 2)
    o = o.reshape(n, nb * blk, h, dh)[:, :L]
    lse = lse.reshape(n, nb * blk, h)[:, :L]
    return o, lse


def dilated_group_attention(q, k, v, window, dilation):
    b, s, h, dh = q.shape
    L = s // dilation

    def to_sub(t):
        return t.reshape(b, L, dilation, h, dh).transpose(0, 2, 1, 3, 4).reshape(b * dilation, L, h, dh)

    o, lse = banded_causal_attention(to_sub(q), to_sub(k), to_sub(v), window // dilation)
    o = o.reshape(b, dilation, L, h, dh).transpose(0, 2, 1, 3, 4).reshape(b, s, h, dh)
    lse = lse.reshape(b, dilation, L, h).transpose(0, 2, 1, 3).reshape(b, s, h)
    return o, lse


def dilated_attention(x, w_qkv, w_o, cos, sin):
    b, s, _ = x.shape
    qkv = (x @ w_qkv).reshape(b, s, A_GROUPS, 3, A_HEADS, HEAD_DIM)
    outs, lses = [], []
    for g, (window, dil) in enumerate(A_PATTERNS):
        q = apply_partial_rope(qkv[:, :, g, 0], cos, sin)
        k = apply_partial_rope(qkv[:, :, g, 1], cos, sin)
        o, lse = dilated_group_attention(q, k, qkv[:, :, g, 2], window, dil)
        outs.append(o)
        lses.append(lse)
    o = jnp.stack(outs)
    wts = jax.nn.softmax(jnp.stack(lses), axis=0)
    mixed = jnp.sum(wts[..., None] * o, axis=0).astype(x.dtype)
    return mixed.reshape(b, s, A_HEADS * HEAD_DIM) @ w_o


def shared_kv(h, w_kv, cos, sin):
    b, s, _ = h.shape
    kv = (h @ w_kv).reshape(b, s, 2, B_HEADS, HEAD_DIM)
    k = apply_partial_rope(kv[:, :, 0], cos, sin)
    v = kv[:, :, 1]
    nblk = -(-s // MOBA_BLOCK)
    pad = nblk * MOBA_BLOCK - s

    def blocks(t):
        t = jnp.pad(t, ((0, 0), (0, pad), (0, 0), (0, 0)))
        return t.reshape(b, nblk, MOBA_BLOCK, B_HEADS, HEAD_DIM).transpose(0, 3, 1, 2, 4)

    kb = blocks(k)
    vb = blocks(v)
    k_mean = jnp.mean(kb.astype(jnp.float32), axis=3).astype(kb.dtype)
    return kb, vb, k_mean


def moba_attention(x, w_q, w_o, kb, vb, k_mean, cos, sin):
    b, s, _ = x.shape
    nblk = kb.shape[2]
    topk = min(MOBA_TOPK, nblk)
    scale = HEAD_DIM ** -0.5
    q = apply_partial_rope((x @ w_q).reshape(b, s, B_HEADS, HEAD_DIM), cos, sin).transpose(0, 2, 1, 3)
    bi = jnp.arange(b)[:, None, None, None]
    hi = jnp.arange(B_HEADS)[None, :, None, None]

    def one_chunk(ci):
        start = ci * MOBA_QCHUNK
        qc = lax.dynamic_slice_in_dim(q, start, MOBA_QCHUNK, axis=2)
        cur = start // MOBA_BLOCK
        gate = jnp.einsum('bhqd,bhnd->bhqn', qc, k_mean).astype(jnp.float32)
        gate = jnp.where(jnp.arange(nblk) < cur, gate, NEG_INF)
        _, sel = lax.top_k(gate, topk)
        k_sel = kb[bi, hi, sel]
        v_sel = vb[bi, hi, sel]
        s_sel = jnp.einsum('bhqd,bhqnkd->bhqnk', qc, k_sel).astype(jnp.float32) * scale
        s_sel = jnp.where((jnp.arange(topk) < cur)[:, None], s_sel, NEG_INF)
        k_own = lax.dynamic_index_in_dim(kb, cur, axis=2, keepdims=False)
        v_own = lax.dynamic_index_in_dim(vb, cur, axis=2, keepdims=False)
        s_own = jnp.einsum('bhqd,bhkd->bhqk', qc, k_own).astype(jnp.float32) * scale
        kpos = cur * MOBA_BLOCK + jnp.arange(MOBA_BLOCK)
        qpos = start + jnp.arange(MOBA_QCHUNK)
        s_own = jnp.where(kpos[None, :] <= qpos[:, None], s_own, NEG_INF)
        scores = jnp.concatenate([s_sel.reshape(b, B_HEADS, MOBA_QCHUNK, topk * MOBA_BLOCK), s_own], axis=-1)
        p = jax.nn.softmax(scores, axis=-1)
        p_sel = p[..., :topk * MOBA_BLOCK].reshape(b, B_HEADS, MOBA_QCHUNK, topk, MOBA_BLOCK)
        p_own = p[..., topk * MOBA_BLOCK:]
        o = (jnp.einsum('bhqnk,bhqnkd->bhqd', p_sel, v_sel.astype(jnp.float32))
             + jnp.einsum('bhqk,bhkd->bhqd', p_own, v_own.astype(jnp.float32)))
        return o.astype(x.dtype)

    o = lax.map(one_chunk, jnp.arange(s // MOBA_QCHUNK, dtype=jnp.int32))
    o = o.transpose(1, 0, 3, 2, 4).reshape(b, s, B_HEADS * HEAD_DIM)
    return o @ w_o


def moe_ffn(x, w_router, b_router, w_gu, b_gu, w_down, b_down):
    b, s, d = x.shape
    t = x.reshape(-1, d)
    n_tok = t.shape[0]
    logits = (t @ w_router + b_router).astype(jnp.float32)
    top_val, top_idx = lax.top_k(logits, TOP_K)
    top_w = jax.nn.softmax(top_val, axis=-1)
    n_slots = n_tok * TOP_K
    slot_e = top_idx.reshape(-1).astype(jnp.int32)
    slot_tok = jnp.repeat(jnp.arange(n_tok, dtype=jnp.int32), TOP_K)
    slot_w = top_w.reshape(-1)
    order = jnp.argsort(slot_e)
    e_sorted = slot_e[order]
    tok_sorted = slot_tok[order]
    w_sorted = slot_w[order]
    counts = jnp.zeros(N_EXPERTS, jnp.int32).at[slot_e].add(1)
    group_start = jnp.cumsum(counts) - counts
    padded = (counts + MOE_BLOCK - 1) // MOE_BLOCK * MOE_BLOCK
    padded_end = jnp.cumsum(padded)
    padded_start = padded_end - padded
    dest = padded_start[e_sorted] + jnp.arange(n_slots, dtype=jnp.int32) - group_start[e_sorted]
    cap = (n_slots + N_EXPERTS * MOE_BLOCK + MOE_BLOCK - 1) // MOE_BLOCK * MOE_BLOCK
    buf_tok = jnp.zeros(cap, jnp.int32).at[dest].set(tok_sorted)
    buf_w = jnp.zeros(cap, jnp.float32).at[dest].set(w_sorted)
    n_blocks = cap // MOE_BLOCK
    block_e = jnp.searchsorted(padded_end, jnp.arange(n_blocks, dtype=jnp.int32) * MOE_BLOCK, side='right')
    block_e = jnp.minimum(block_e, N_EXPERTS - 1).astype(jnp.int32)

    def expert_block(args):
        e, tok = args
        gu = t[tok] @ w_gu[e] + b_gu[e]
        gate = jnp.minimum(gu[:, :D_FF], SWIGLU_LIMIT)
        lin = jnp.clip(gu[:, D_FF:], -SWIGLU_LIMIT, SWIGLU_LIMIT)
        hid = (lin + 1.0) * gate * jax.nn.sigmoid(SWIGLU_ALPHA * gate)
        return hid @ w_down[e] + b_down[e]

    y = lax.map(expert_block, (block_e, buf_tok.reshape(n_blocks, MOE_BLOCK)))
    y = y.reshape(cap, d).astype(jnp.float32) * buf_w[:, None]
    out = jnp.zeros((n_tok, d), jnp.float32).at[buf_tok].add(y)
    return out.astype(x.dtype).reshape(b, s, d)


def setup_inputs(seed: int = 0) -> dict:
    key = jax.random.key(seed)
    ks = jax.random.split(key, 18)
    f32 = jnp.float32
    d = D_MODEL
    a_width = A_HEADS * HEAD_DIM
    b_width = B_HEADS * HEAD_DIM
    x = jax.random.normal(ks[0], (BATCH, SEQ, d), f32)
    qkv_scale = jnp.array([1.0, 1.0, DEEPNORM_BETA], f32)[:, None, None]
    a_w_qkv = (jax.random.normal(ks[1], (N_A_LAYERS, d, A_GROUPS, 3, A_HEADS, HEAD_DIM), f32)
               * (d ** -0.5) * qkv_scale).reshape(N_A_LAYERS, d, A_GROUPS * 3 * a_width)
    a_w_o = jax.random.normal(ks[2], (N_A_LAYERS, a_width, d), f32) * (a_width ** -0.5) * DEEPNORM_BETA
    kv_scale = jnp.array([1.0, DEEPNORM_BETA], f32)[:, None, None]
    kv_w = (jax.random.normal(ks[3], (d, 2, B_HEADS, HEAD_DIM), f32) * (d ** -0.5) * kv_scale).reshape(d, 2 * b_width)
    b_w_q = jax.random.normal(ks[4], (N_B_LAYERS, d, b_width), f32) * (d ** -0.5)
    b_w_o = jax.random.normal(ks[5], (N_B_LAYERS, b_width, d), f32) * (b_width ** -0.5) * DEEPNORM_BETA
    router_w = jax.random.normal(ks[6], (DEPTH, d, N_EXPERTS), f32) * (d ** -0.5)
    router_b = jax.random.normal(ks[7], (DEPTH, N_EXPERTS), f32) * 0.01
    moe_w_gate_up = jax.random.normal(ks[8], (DEPTH, N_EXPERTS, d, 2 * D_FF), f32) * (d ** -0.5)
    moe_b_gate_up = jax.random.normal(ks[9], (DEPTH, N_EXPERTS, 2 * D_FF), f32) * 0.01
    moe_w_down = jax.random.normal(ks[10], (DEPTH, N_EXPERTS, D_FF, d), f32) * (D_FF ** -0.5) * DEEPNORM_BETA
    moe_b_down = jax.random.normal(ks[11], (DEPTH, N_EXPERTS, d), f32) * 0.01
    ln1_g = 1.0 + 0.02 * jax.random.normal(ks[12], (DEPTH, d), f32)
    ln1_b = 0.02 * jax.random.normal(ks[13], (DEPTH, d), f32)
    ln2_g = 1.0 + 0.02 * jax.random.normal(ks[14], (DEPTH, d), f32)
    ln2_b = 0.02 * jax.random.normal(ks[15], (DEPTH, d), f32)
    return {'x': x, 'a_w_qkv': a_w_qkv, 'a_w_o': a_w_o, 'kv_w': kv_w, 'b_w_q': b_w_q, 'b_w_o': b_w_o,
            'router_w': router_w, 'router_b': router_b, 'moe_w_gate_up': moe_w_gate_up,
            'moe_b_gate_up': moe_b_gate_up, 'moe_w_down': moe_w_down, 'moe_b_down': moe_b_down,
            'ln1_g': ln1_g, 'ln1_b': ln1_b, 'ln2_g': ln2_g, 'ln2_b': ln2_b}


def reference(x, a_w_qkv, a_w_o, kv_w, b_w_q, b_w_o, router_w, router_b, moe_w_gate_up,
              moe_b_gate_up, moe_w_down, moe_b_down, ln1_g, ln1_b, ln2_g, ln2_b):
    s = x.shape[1]
    cos, sin = rope_tables(s)
    h = x
    kb = vb = k_mean = None
    for layer in range(DEPTH):
        if layer < N_A_LAYERS:
            mix = dilated_attention(h, a_w_qkv[layer], a_w_o[layer], cos, sin)
        else:
            if layer == N_A_LAYERS:
                kb, vb, k_mean = shared_kv(h, kv_w, cos, sin)
            j = layer - N_A_LAYERS
            mix = moba_attention(h, b_w_q[j], b_w_o[j], kb, vb, k_mean, cos, sin)
        h = layer_norm(DEEPNORM_ALPHA * h + mix, ln1_g[layer], ln1_b[layer])
        ffn = moe_ffn(h, router_w[layer], router_b[layer], moe_w_gate_up[layer], moe_b_gate_up[layer],
                      moe_w_down[layer], moe_b_down[layer])
        h = layer_norm(DEEPNORM_ALPHA * h + ffn, ln2_g[layer], ln2_b[layer])
    return h
```

```python
import functools

import jax
import jax.numpy as jnp
from jax import lax
from jax.experimental import pallas as pl
from jax.experimental.pallas import tpu as pltpu

D_MODEL = 2048
SEQ = 2048
DEPTH = 2
HEAD_DIM = 128
ROT_DIM = HEAD_DIM // 4
ROPE_THETA = 500000.0
A_PATTERNS = ((128, 1), (512, 4), (2048, 16))
A_GROUPS = len(A_PATTERNS)
A_HEADS = 16
B_HEADS = 16
MOBA_BLOCK = 256
MOBA_TOPK = 3
N_EXPERTS = 32
TOP_K = 4
D_FF = 2048
SWIGLU_LIMIT = 7.0
SWIGLU_ALPHA = 1.702
N_A_LAYERS = DEPTH // 2
DEEPNORM_ALPHA = (2 * DEPTH) ** 0.25
LN_EPS = 1e-5
NEG_INF = -1e30

LANES = 128
V7X_VMEM_BYTES = 64 * 1024 * 1024
VMEM_LIMIT = 56 * 1024 * 1024

BF16 = jnp.bfloat16
F32 = jnp.float32


def _cparams(n_axes):
    return pltpu.CompilerParams(dimension_semantics=("arbitrary",) * n_axes, vmem_limit_bytes=VMEM_LIMIT)


def _rope_tables(seq):
    half = ROT_DIM // 2
    inv = ROPE_THETA ** (-jnp.arange(0, ROT_DIM, 2, dtype=F32) / ROT_DIM)
    ang = jnp.arange(seq, dtype=F32)[:, None] * inv[None, :]
    cos, sin = jnp.cos(ang), jnp.sin(ang)
    zeros = jnp.zeros((seq, HEAD_DIM - ROT_DIM), F32)
    c = jnp.concatenate([cos, cos, jnp.ones((seq, HEAD_DIM - ROT_DIM), F32)], axis=1)
    s1 = jnp.concatenate([-sin, jnp.zeros((seq, half), F32), zeros], axis=1)
    s2 = jnp.concatenate([jnp.zeros((seq, half), F32), sin, zeros], axis=1)
    return jnp.stack([c, s1, s2])


def _rope_tile(x, c, s1, s2):
    half = ROT_DIM // 2
    return x * c + pltpu.roll(x, HEAD_DIM - half, 1) * s1 + pltpu.roll(x, half, 1) * s2


def _proj_kernel(a_ref, w_ref, tab_ref, o_ref, *, tn, seg_cols, rope_mod, rope_cnt):
    j = pl.program_id(1)
    acc = jnp.dot(a_ref[...], w_ref[...].astype(BF16), preferred_element_type=F32)
    seg = (j * tn) // seg_cols
    do_rope = (seg % rope_mod) < rope_cnt

    @pl.when(do_rope)
    def _():
        c, s1, s2 = tab_ref[0], tab_ref[1], tab_ref[2]
        for h in range(tn // HEAD_DIM):
            sl = slice(h * HEAD_DIM, (h + 1) * HEAD_DIM)
            o_ref[:, sl] = _rope_tile(acc[:, sl], c, s1, s2).astype(o_ref.dtype)

    @pl.when(jnp.logical_not(do_rope))
    def _():
        o_ref[...] = acc.astype(o_ref.dtype)


def _proj(a, w, tabs, *, col0, n_out, pos_period, rope_mod, rope_cnt, tm=1024, tn=512):
    m, k = a.shape
    assert m % tm == 0 and n_out % tn == 0 and col0 % tn == 0 and pos_period % tm == 0
    jb = col0 // tn
    per = pos_period // tm
    kern = functools.partial(_proj_kernel, tn=tn, seg_cols=D_MODEL, rope_mod=rope_mod, rope_cnt=rope_cnt)
    return pl.pallas_call(
        kern,
        grid=(m // tm, n_out // tn),
        in_specs=[
            pl.BlockSpec((tm, k), lambda i, j: (i, 0)),
            pl.BlockSpec((k, tn), lambda i, j: (0, j + jb)),
            pl.BlockSpec((3, tm, HEAD_DIM), lambda i, j: (0, i % per, 0)),
        ],
        out_specs=pl.BlockSpec((tm, tn), lambda i, j: (i, j)),
        out_shape=jax.ShapeDtypeStruct((m, n_out), BF16),
        compiler_params=_cparams(2),
        name="proj_rope",
    )(a, w, tabs)


def _band_attn_kernel(*refs, has_prev, n_heads, blk):
    if has_prev:
        q_ref, kp_ref, kc_ref, vp_ref, vc_ref, o_ref, lse_ref = refs
    else:
        q_ref, kc_ref, vc_ref, o_ref, lse_ref = refs
    i = pl.program_id(1)
    nk = 2 * blk if has_prev else blk
    qi = lax.broadcasted_iota(jnp.int32, (blk, nk), 0)
    ki = lax.broadcasted_iota(jnp.int32, (blk, nk), 1) - (nk - blk)
    diff = qi - ki
    mask = (diff >= 0) & (diff <= blk)
    if has_prev:
        mask = mask & ((i > 0) | (ki >= 0))
    scale = HEAD_DIM ** -0.5
    lane = lax.broadcasted_iota(jnp.int32, (blk, n_heads), 1)
    lse_all = jnp.zeros((blk, n_heads), F32)
    for h in range(n_heads):
        sl = slice(h * HEAD_DIM, (h + 1) * HEAD_DIM)
        q = q_ref[:, sl]
        if has_prev:
            k = jnp.concatenate([kp_ref[:, sl], kc_ref[:, sl]], axis=0)
            v = jnp.concatenate([vp_ref[:, sl], vc_ref[:, sl]], axis=0)
        else:
            k, v = kc_ref[:, sl], vc_ref[:, sl]
        s = lax.dot_general(q, k, (((1,), (1,)), ((), ())), preferred_element_type=F32) * scale
        s = jnp.where(mask, s, NEG_INF)
        m = jnp.max(s, axis=1, keepdims=True)
        p = jnp.exp(s - m)
        den = jnp.sum(p, axis=1, keepdims=True)
        o = jnp.dot(p.astype(BF16), v, preferred_element_type=F32) / den
        o_ref[:, sl] = o.astype(o_ref.dtype)
        lse_all = jnp.where(lane == h, m + jnp.log(den), lse_all)
    lse_ref[...] = lse_all


def _band_attn(qkv, n_heads=A_HEADS, blk=128):
    n, L, c3 = qkv.shape
    c = c3 // 3
    assert c == n_heads * HEAD_DIM and L % blk == 0
    nb = L // blk
    has_prev = nb > 1
    spec = lambda col, prev: pl.BlockSpec(
        (None, blk, c), (lambda s, i: (s, jnp.maximum(i - 1, 0), col)) if prev else (lambda s, i: (s, i, col)))
    if has_prev:
        in_specs = [spec(0, False), spec(1, True), spec(1, False), spec(2, True), spec(2, False)]
        args = (qkv,) * 5
    else:
        in_specs = [spec(0, False), spec(1, False), spec(2, False)]
        args = (qkv,) * 3
    kern = functools.partial(_band_attn_kernel, has_prev=has_prev, n_heads=n_heads, blk=blk)
    return pl.pallas_call(
        kern,
        grid=(n, nb),
        in_specs=in_specs,
        out_specs=[pl.BlockSpec((None, blk, c), lambda s, i: (s, i, 0)),
                   pl.BlockSpec((None, blk, n_heads), lambda s, i: (s, i, 0))],
        out_shape=[jax.ShapeDtypeStruct((n, L, c), BF16), jax.ShapeDtypeStruct((n, L, n_heads), F32)],
        compiler_params=_cparams(2),
        name="band_attn",
    )(*args)


def _mix_kernel(o0, o1, o2, l0, l1, l2, out_ref, *, n_heads):
    ls = [l0[...], l1[...], l2[...]]
    mx = jnp.maximum(jnp.maximum(ls[0], ls[1]), ls[2])
    es = [jnp.exp(l - mx) for l in ls]
    tot = es[0] + es[1] + es[2]
    ws = [e / tot for e in es]
    os_ = [o0, o1, o2]
    for h in range(n_heads):
        sl = slice(h * HEAD_DIM, (h + 1) * HEAD_DIM)
        acc = ws[0][:, h:h + 1] * os_[0][:, sl].astype(F32)
        acc += ws[1][:, h:h + 1] * os_[1][:, sl].astype(F32)
        acc += ws[2][:, h:h + 1] * os_[2][:, sl].astype(F32)
        out_ref[:, sl] = acc.astype(out_ref.dtype)


def _mix(os_, ls, tm=256, n_heads=A_HEADS):
    m, c = os_[0].shape
    ospec = pl.BlockSpec((tm, c), lambda i: (i, 0))
    lspec = pl.BlockSpec((tm, n_heads), lambda i: (i, 0))
    return pl.pallas_call(
        functools.partial(_mix_kernel, n_heads=n_heads),
        grid=(m // tm,),
        in_specs=[ospec] * 3 + [lspec] * 3,
        out_specs=ospec,
        out_shape=jax.ShapeDtypeStruct((m, c), BF16),
        compiler_params=_cparams(1),
        name="group_mix",
    )(*os_, *ls)


def _layer_norm_rows(y, g, b):
    mu = jnp.mean(y, axis=1, keepdims=True)
    yc = y - mu
    var = jnp.mean(yc * yc, axis=1, keepdims=True)
    return yc * lax.rsqrt(var + LN_EPS) * g + b


def _oproj_ln_kernel(a_ref, w_ref, h_ref, g_ref, b_ref, of_ref, ob_ref):
    acc = jnp.dot(a_ref[...], w_ref[...], preferred_element_type=F32)
    y = _layer_norm_rows(DEEPNORM_ALPHA * h_ref[...] + acc, g_ref[...], b_ref[...])
    of_ref[...] = y
    ob_ref[...] = y.astype(ob_ref.dtype)


def _oproj_ln(a, w_bf, h, g, b, tm=512):
    m, k = a.shape
    d = w_bf.shape[1]
    row = lambda i: (i, 0)
    const = lambda i: (0, 0)
    return pl.pallas_call(
        _oproj_ln_kernel,
        grid=(m // tm,),
        in_specs=[pl.BlockSpec((tm, k), row), pl.BlockSpec((k, d), const), pl.BlockSpec((tm, d), row),
                  pl.BlockSpec((1, d), const), pl.BlockSpec((1, d), const)],
        out_specs=[pl.BlockSpec((tm, d), row), pl.BlockSpec((tm, d), row)],
        out_shape=[jax.ShapeDtypeStruct((m, d), F32), jax.ShapeDtypeStruct((m, d), BF16)],
        compiler_params=_cparams(1),
        name="oproj_ln",
    )(a, w_bf, h, g.reshape(1, d), b.reshape(1, d))


def _moba_kernel(q_ref, k_ref, v_ref, o_ref, *, heads, seq, blk):
    c = pl.program_id(2)
    nblk = seq // blk
    scale = HEAD_DIM ** -0.5
    lane = lax.broadcasted_iota(jnp.int32, (blk, LANES), 1)
    qi = lax.broadcasted_iota(jnp.int32, (blk, blk), 0)
    ki = lax.broadcasted_iota(jnp.int32, (blk, blk), 1)
    causal_bias = jnp.where(ki <= qi, 0.0, NEG_INF).astype(F32)
    for h in range(heads):
        sl = slice(h * HEAD_DIM, (h + 1) * HEAD_DIM)
        q = q_ref[:, sl]
        k = k_ref[:, sl]
        v = v_ref[:, sl]
        kmean = jnp.mean(k.astype(F32).reshape(nblk, blk, HEAD_DIM), axis=1)
        kmean = jnp.concatenate([kmean, jnp.zeros((LANES - nblk, HEAD_DIM), F32)], axis=0).astype(BF16)
        gate = lax.dot_general(q, kmean, (((1,), (1,)), ((), ())), preferred_element_type=F32)
        gate = jnp.where(lane < c, gate, NEG_INF)
        cnt = jnp.zeros((blk, LANES), jnp.int32)
        for n2 in range(nblk):
            g2 = gate[:, n2:n2 + 1]
            ahead = (g2 > gate) | ((g2 == gate) & (n2 < lane))
            cnt = cnt + ahead.astype(jnp.int32)
        sel = (cnt < MOBA_TOPK) & (lane < c)
        s = lax.dot_general(q, k, (((1,), (1,)), ((), ())), preferred_element_type=F32) * scale
        sel_bias = jnp.where(sel, 0.0, NEG_INF).astype(F32)
        parts = []
        for n2 in range(nblk):
            bias = jnp.where(c == n2, causal_bias, sel_bias[:, n2:n2 + 1])
            parts.append(s[:, n2 * blk:(n2 + 1) * blk] + bias)
        s = jnp.concatenate(parts, axis=1)
        m = jnp.max(s, axis=1, keepdims=True)
        p = jnp.exp(s - m)
        den = jnp.sum(p, axis=1, keepdims=True)
        o = jnp.dot(p.astype(BF16), v, preferred_element_type=F32) / den
        o_ref[:, sl] = o.astype(o_ref.dtype)


def _moba(q, kv, heads_per_step=4, blk=MOBA_BLOCK):
    b, s, c = q.shape
    n_heads = c // HEAD_DIM
    hw = heads_per_step * HEAD_DIM
    ng = n_heads // heads_per_step
    kern = functools.partial(_moba_kernel, heads=heads_per_step, seq=s, blk=blk)
    return pl.pallas_call(
        kern,
        grid=(b, ng, s // blk),
        in_specs=[pl.BlockSpec((None, blk, hw), lambda bi, g, t: (bi, t, g)),
                  pl.BlockSpec((None, s, hw), lambda bi, g, t: (bi, 0, g)),
                  pl.BlockSpec((None, s, hw), lambda bi, g, t: (bi, 0, g + ng))],
        out_specs=pl.BlockSpec((None, blk, hw), lambda bi, g, t: (bi, t, g)),
        out_shape=jax.ShapeDtypeStruct((b, s, c), BF16),
        compiler_params=_cparams(3),
        name="moba_attn",
    )(q, kv, kv)


def _router_kernel(h_ref, wr_ref, br_ref, idx_ref, w_ref, rank_ref, cnt_ref, *, tm):
    t = pl.program_id(0)

    @pl.when(t == 0)
    def _():
        cnt_ref[...] = jnp.zeros_like(cnt_ref)

    logits = lax.dot_general(wr_ref[...], h_ref[...], (((1,), (1,)), ((), ())),
                             preferred_element_type=F32) + br_ref[...]
    row = lax.broadcasted_iota(jnp.int32, (N_EXPERTS, tm), 0)
    rem = logits
    vals, idxs, hots = [], [], []
    for _ in range(TOP_K):
        mx = jnp.max(rem, axis=0, keepdims=True)
        ix = jnp.min(jnp.where(rem == mx, row, N_EXPERTS), axis=0, keepdims=True)
        hot = row == ix
        vals.append(mx)
        idxs.append(ix)
        hots.append(hot)
        rem = jnp.where(hot, -jnp.inf, rem)
    es = [jnp.exp(v - vals[0]) for v in vals]
    tot = es[0] + es[1] + es[2] + es[3]
    sel = jnp.zeros((N_EXPERTS, tm), F32)
    for hot in hots:
        sel = sel + hot.astype(F32)
    ri = lax.broadcasted_iota(jnp.int32, (tm, tm), 0)
    ci = lax.broadcasted_iota(jnp.int32, (tm, tm), 1)
    upper = jnp.where(ri <= ci, 1.0, 0.0).astype(BF16)
    incl = jnp.dot(sel.astype(BF16), upper, preferred_element_type=F32)
    base = cnt_ref[:, 0:1]
    rank_e = base + incl - sel
    ranks = [jnp.sum(jnp.where(hot, rank_e, 0.0), axis=0, keepdims=True) for hot in hots]
    idx_ref[...] = jnp.concatenate(idxs, axis=0)
    w_ref[...] = jnp.concatenate([e / tot for e in es], axis=0)
    rank_ref[...] = jnp.concatenate(ranks, axis=0).astype(jnp.int32)
    cnt_ref[...] = jnp.broadcast_to(base + incl[:, tm - 1:tm], cnt_ref.shape)


def _router(h_bf, w_router, b_router, tm=512):
    n_tok, d = h_bf.shape
    wr_t = w_router.T.astype(BF16)
    br = b_router.reshape(N_EXPERTS, 1).astype(F32)
    tok = lambda i: (0, i)
    const = lambda i: (0, 0)
    return pl.pallas_call(
        functools.partial(_router_kernel, tm=tm),
        grid=(n_tok // tm,),
        in_specs=[pl.BlockSpec((tm, d), lambda i: (i, 0)), pl.BlockSpec((N_EXPERTS, d), const),
                  pl.BlockSpec((N_EXPERTS, 1), const)],
        out_specs=[pl.BlockSpec((TOP_K, tm), tok), pl.BlockSpec((TOP_K, tm), tok), pl.BlockSpec((TOP_K, tm), tok),
                   pl.BlockSpec((N_EXPERTS, LANES), const)],
        out_shape=[jax.ShapeDtypeStruct((TOP_K, n_tok), jnp.int32), jax.ShapeDtypeStruct((TOP_K, n_tok), F32),
                   jax.ShapeDtypeStruct((TOP_K, n_tok), jnp.int32), jax.ShapeDtypeStruct((N_EXPERTS, LANES), F32)],
        compiler_params=_cparams(1),
        name="moe_router",
    )(h_bf, wr_t, br)


SUB = 256
SUPER = 1024
SUBS_PER_SUPER = SUPER // SUB


def _moe_tables(counts, n_slots):
    n_super_max = N_EXPERTS + n_slots // SUPER
    n_sub_e = (counts + SUB - 1) // SUB
    xs_start = SUB * (jnp.cumsum(n_sub_e) - n_sub_e)
    n_sb_e = (counts + SUPER - 1) // SUPER
    sb_end = jnp.cumsum(n_sb_e)
    sb_start = sb_end - n_sb_e
    total = sb_end[-1]
    g = jnp.arange(n_super_max, dtype=jnp.int32)
    gc = jnp.minimum(g, total - 1)
    e_of = jnp.minimum(jnp.searchsorted(sb_end, gc, side="right"), N_EXPERTS - 1).astype(jnp.int32)
    j_in = gc - sb_start[e_of]
    blk0 = xs_start[e_of] // SUB + SUBS_PER_SUPER * j_in
    nsub = jnp.clip(n_sub_e[e_of] - SUBS_PER_SUPER * j_in, 0, SUBS_PER_SUPER)
    nsub_active = jnp.where(g < total, nsub, 0)
    ys_start = SUPER * sb_start
    return (xs_start.astype(jnp.int32), ys_start.astype(jnp.int32), e_of, blk0.astype(jnp.int32),
            nsub.astype(jnp.int32), nsub_active.astype(jnp.int32), gc.astype(jnp.int32))


def _dispatch_kernel(start_ref, idx_ref, rank_ref, h_hbm, xs_in, xs_hbm, sem, *, tm):
    del xs_in
    t0 = pl.program_id(0) * tm

    def row_copy(t, k):
        dst = start_ref[idx_ref[k, t]] + rank_ref[k, t]
        return pltpu.make_async_copy(h_hbm.at[pl.ds(t0 + t, 1)], xs_hbm.at[pl.ds(dst, 1)], sem)

    def issue(t, carry):
        for k in range(TOP_K):
            row_copy(t, k).start()
        return carry

    def drain(t, carry):
        for k in range(TOP_K):
            row_copy(t, k).wait()
        return carry

    lax.fori_loop(0, tm, issue, 0)
    lax.fori_loop(0, tm, drain, 0)


def _dispatch(h, top_idx, rank, xs_start, n_rows, tm=256):
    n_tok, d = h.shape
    xs0 = jnp.zeros((n_rows, d), h.dtype)
    smem_blk = lambda: pl.BlockSpec((TOP_K, tm), lambda i, s: (0, i), memory_space=pltpu.SMEM)
    grid_spec = pltpu.PrefetchScalarGridSpec(
        num_scalar_prefetch=1,
        grid=(n_tok // tm,),
        in_specs=[smem_blk(), smem_blk(), pl.BlockSpec(memory_space=pl.ANY), pl.BlockSpec(memory_space=pl.ANY)],
        out_specs=pl.BlockSpec(memory_space=pl.ANY),
        scratch_shapes=[pltpu.SemaphoreType.DMA],
    )
    return pl.pallas_call(
        functools.partial(_dispatch_kernel, tm=tm),
        grid_spec=grid_spec,
        out_shape=jax.ShapeDtypeStruct((n_rows, d), h.dtype),
        input_output_aliases={4: 0},
        compiler_params=_cparams(1),
        name="moe_dispatch",
    )(xs_start, top_idx, rank, h, xs0)


def _ffn_kernel(e_ref, blk0_ref, nsub_ref, act_ref, out_ref_idx, x0, x1, x2, x3, wg_ref, wu_ref, bg_ref, bu_ref,
                wd_ref, bd_ref, y_ref, wgu_s, wd_s, *, tf):
    del e_ref, blk0_ref, nsub_ref, out_ref_idx
    g = pl.program_id(0)
    j = pl.program_id(1)
    nact = act_ref[g]

    @pl.when(nact > 0)
    def _():
        wgu_s[:, :tf] = wg_ref[...].astype(BF16)
        wgu_s[:, tf:] = wu_ref[...].astype(BF16)
        wd_s[...] = wd_ref[...].astype(BF16)
        for s, x_ref in enumerate((x0, x1, x2, x3)):
            @pl.when(s < nact)
            def _():
                x = x_ref[...].astype(BF16)
                gu = jnp.dot(x, wgu_s[...], preferred_element_type=F32)
                gate = jnp.minimum(gu[:, :tf] + bg_ref[...], SWIGLU_LIMIT)
                lin = jnp.clip(gu[:, tf:] + bu_ref[...], -SWIGLU_LIMIT, SWIGLU_LIMIT)
                hid = (lin + 1.0) * gate * jax.nn.sigmoid(SWIGLU_ALPHA * gate)
                y = jnp.dot(hid.astype(BF16), wd_s[...], preferred_element_type=F32)
                rows = slice(s * SUB, (s + 1) * SUB)

                @pl.when(j == 0)
                def _():
                    y_ref[rows, :] = y + bd_ref[...]

                @pl.when(j > 0)
                def _():
                    y_ref[rows, :] += y


def _expert_ffn(xs, tables, layer, w_gu, b_gu, w_down, b_down, n_super_max, tf=256):
    _, _, e_of, blk0, nsub, nsub_active, g_out = tables
    n_rows, d = xs.shape
    n_we = w_gu.shape[0] * N_EXPERTS
    e_of = e_of + layer * N_EXPERTS
    w_gu = w_gu.reshape(n_we, d, 2 * D_FF)
    w_down = w_down.reshape(n_we, D_FF, d)
    b_gu = b_gu.reshape(n_we, 1, 2 * D_FF)
    b_down = b_down.reshape(n_we, 1, d)
    n_ff = D_FF // tf
    last = n_ff - 1

    def x_spec(s):
        return pl.BlockSpec(
            (SUB, d), lambda g, j, e, b0, ns, na, go: (b0[g] + jnp.minimum(s, jnp.maximum(ns[g], 1) - 1), 0))

    def jj(j, na, g):
        return jnp.where(na[g] > 0, j, last)

    in_specs = [x_spec(s) for s in range(SUBS_PER_SUPER)] + [
        pl.BlockSpec((None, d, tf), lambda g, j, e, b0, ns, na, go: (e[g], 0, jj(j, na, g))),
        pl.BlockSpec((None, d, tf), lambda g, j, e, b0, ns, na, go: (e[g], 0, n_ff + jj(j, na, g))),
        pl.BlockSpec((None, 1, tf), lambda g, j, e, b0, ns, na, go: (e[g], 0, jj(j, na, g))),
        pl.BlockSpec((None, 1, tf), lambda g, j, e, b0, ns, na, go: (e[g], 0, n_ff + jj(j, na, g))),
        pl.BlockSpec((None, tf, d), lambda g, j, e, b0, ns, na, go: (e[g], jj(j, na, g), 0)),
        pl.BlockSpec((None, 1, d), lambda g, j, e, b0, ns, na, go: (e[g], 0, 0)),
    ]
    grid_spec = pltpu.PrefetchScalarGridSpec(
        num_scalar_prefetch=5,
        grid=(n_super_max, n_ff),
        in_specs=in_specs,
        out_specs=pl.BlockSpec((SUPER, d), lambda g, j, e, b0, ns, na, go: (go[g], 0)),
        scratch_shapes=[pltpu.VMEM((d, 2 * tf), BF16), pltpu.VMEM((tf, d), BF16)],
    )
    return pl.pallas_call(
        functools.partial(_ffn_kernel, tf=tf),
        grid_spec=grid_spec,
        out_shape=jax.ShapeDtypeStruct((n_super_max * SUPER, d), F32),
        compiler_params=_cparams(2),
        name="moe_ffn",
    )(e_of, blk0, nsub, nsub_active, g_out, xs, xs, xs, xs, w_gu, w_gu, b_gu, b_gu, w_down, b_down)


def _combine_kernel(start_ref, idx_ref, rank_ref, w_ref, h_ref, g_ref, b_ref, ys_hbm, of_ref, ob_ref, ybuf, sem,
                    *, tm):
    def row_copy(t, k):
        src = start_ref[idx_ref[k, t]] + rank_ref[k, t]
        return pltpu.make_async_copy(ys_hbm.at[pl.ds(src, 1)], ybuf.at[k, pl.ds(t, 1)], sem)

    def issue(t, carry):
        for k in range(TOP_K):
            row_copy(t, k).start()
        return carry

    def drain(t, carry):
        for k in range(TOP_K):
            row_copy(t, k).wait()
        return carry

    lax.fori_loop(0, tm, issue, 0)
    lax.fori_loop(0, tm, drain, 0)
    w = w_ref[...]
    ffn = w[:, 0:1] * ybuf[0]
    for k in range(1, TOP_K):
        ffn = ffn + w[:, k:k + 1] * ybuf[k]
    y = _layer_norm_rows(DEEPNORM_ALPHA * h_ref[...] + ffn, g_ref[...], b_ref[...])
    of_ref[...] = y
    ob_ref[...] = y.astype(ob_ref.dtype)


def _combine_ln(ys, top_idx, rank, w_t, ys_start, h, g, b, tm=128):
    n_tok, d = h.shape
    smem_blk = lambda: pl.BlockSpec((TOP_K, tm), lambda i, s: (0, i), memory_space=pltpu.SMEM)
    row = lambda i, s: (i, 0)
    const = lambda i, s: (0, 0)
    grid_spec = pltpu.PrefetchScalarGridSpec(
        num_scalar_prefetch=1,
        grid=(n_tok // tm,),
        in_specs=[smem_blk(), smem_blk(), pl.BlockSpec((tm, TOP_K), row), pl.BlockSpec((tm, d), row),
                  pl.BlockSpec((1, d), const), pl.BlockSpec((1, d), const), pl.BlockSpec(memory_space=pl.ANY)],
        out_specs=[pl.BlockSpec((tm, d), row), pl.BlockSpec((tm, d), row)],
        scratch_shapes=[pltpu.VMEM((TOP_K, tm, d), F32), pltpu.SemaphoreType.DMA],
    )
    return pl.pallas_call(
        functools.partial(_combine_kernel, tm=tm),
        grid_spec=grid_spec,
        out_shape=[jax.ShapeDtypeStruct((n_tok, d), F32), jax.ShapeDtypeStruct((n_tok, d), BF16)],
        compiler_params=_cparams(1),
        name="moe_combine_ln",
    )(ys_start, top_idx, rank, w_t, h, g.reshape(1, d), b.reshape(1, d), ys)


def _moe_ln(h, h_bf, layer, w_router, b_router, w_gu, b_gu, w_down, b_down, ln_g, ln_b):
    n_tok, d = h.shape
    n_slots = n_tok * TOP_K
    top_idx, top_w, rank, cnt = _router(h_bf, w_router, b_router)
    counts = cnt[:, 0].astype(jnp.int32)
    tables = _moe_tables(counts, n_slots)
    xs_start, ys_start = tables[0], tables[1]
    n_super_max = N_EXPERTS + n_slots // SUPER
    n_rows = n_slots + N_EXPERTS * SUB
    xs = _dispatch(h, top_idx, rank, xs_start, n_rows)
    ys = _expert_ffn(xs, tables, layer, w_gu, b_gu, w_down, b_down, n_super_max)
    return _combine_ln(ys, top_idx, rank, top_w.T, ys_start, h, ln_g, ln_b)


def _dilated_mix(x3, w_qkv, tabs):
    b, s, d = x3.shape
    n_tok = b * s
    width = A_HEADS * HEAD_DIM
    outs, lses = [], []
    for g, (window, dil) in enumerate(A_PATTERNS):
        assert window // dil == 128
        L = s // dil
        xp = x3.reshape(b, L, dil, d).transpose(0, 2, 1, 3).reshape(n_tok, d).astype(BF16)
        tabs_p = tabs.reshape(3, L, dil, HEAD_DIM).transpose(0, 2, 1, 3).reshape(3, s, HEAD_DIM)
        qkv = _proj(xp, w_qkv, tabs_p, col0=g * 3 * width, n_out=3 * width, pos_period=s, rope_mod=3, rope_cnt=2)
        o, lse = _band_attn(qkv.reshape(b * dil, L, 3 * width))
        outs.append(o.reshape(b, dil, L, width).transpose(0, 2, 1, 3).reshape(n_tok, width))
        lses.append(lse.reshape(b, dil, L, A_HEADS).transpose(0, 2, 1, 3).reshape(n_tok, A_HEADS))
    return _mix(outs, lses)


def kernel(x, a_w_qkv, a_w_o, kv_w, b_w_q, b_w_o, router_w, router_b, moe_w_gate_up, moe_b_gate_up, moe_w_down,
           moe_b_down, ln1_g, ln1_b, ln2_g, ln2_b):
    b, s, d = x.shape
    n_tok = b * s
    tabs = _rope_tables(s)
    h = x.reshape(n_tok, d)
    h_bf = h.astype(BF16)
    kv = None
    for layer in range(DEPTH):
        if layer < N_A_LAYERS:
            mix = _dilated_mix(h.reshape(b, s, d), a_w_qkv[layer], tabs)
            w_o = a_w_o[layer]
        else:
            j = layer - N_A_LAYERS
            if layer == N_A_LAYERS:
                kv = _proj(h_bf, kv_w, tabs, col0=0, n_out=2 * B_HEADS * HEAD_DIM, pos_period=s, rope_mod=2,
                           rope_cnt=1)
            q = _proj(h_bf, b_w_q[j], tabs, col0=0, n_out=B_HEADS * HEAD_DIM, pos_period=s, rope_mod=1, rope_cnt=1)
            mix = _moba(q.reshape(b, s, -1), kv.reshape(b, s, -1)).reshape(n_tok, -1)
            w_o = b_w_o[j]
        h, h_bf = _oproj_ln(mix, w_o.astype(BF16), h, ln1_g[layer], ln1_b[layer])
        h, h_bf = _moe_ln(h, h_bf, layer, router_w[layer], router_b[layer], moe_w_gate_up, moe_b_gate_up,
                          moe_w_down, moe_b_down, ln2_g[layer], ln2_b[layer])
    return h.reshape(b, s, d)
```

```python
import functools

import jax
import jax.numpy as jnp
from jax import lax
from jax.experimental import pallas as pl
from jax.experimental.pallas import tpu as pltpu

D_MODEL = 2048
SEQ = 2048
DEPTH = 2
HEAD_DIM = 128
ROT_DIM = HEAD_DIM // 4
ROPE_THETA = 500000.0
A_PATTERNS = ((128, 1), (512, 4), (2048, 16))
A_GROUPS = len(A_PATTERNS)
A_HEADS = 16
B_HEADS = 16
MOBA_BLOCK = 256
MOBA_TOPK = 3
N_EXPERTS = 32
TOP_K = 4
D_FF = 2048
SWIGLU_LIMIT = 7.0
SWIGLU_ALPHA = 1.702
N_A_LAYERS = DEPTH // 2
DEEPNORM_ALPHA = (2 * DEPTH) ** 0.25
LN_EPS = 1e-5
NEG_INF = -1e30

LANES = 128
V7X_VMEM_BYTES = 64 * 1024 * 1024
VMEM_LIMIT = 56 * 1024 * 1024

BF16 = jnp.bfloat16
F32 = jnp.float32


def _cparams(n_axes):
    return pltpu.CompilerParams(dimension_semantics=("arbitrary",) * n_axes, vmem_limit_bytes=VMEM_LIMIT)


def _rope_tables(seq):
    half = ROT_DIM // 2
    inv = ROPE_THETA ** (-jnp.arange(0, ROT_DIM, 2, dtype=F32) / ROT_DIM)
    ang = jnp.arange(seq, dtype=F32)[:, None] * inv[None, :]
    cos, sin = jnp.cos(ang), jnp.sin(ang)
    zeros = jnp.zeros((seq, HEAD_DIM - ROT_DIM), F32)
    c = jnp.concatenate([cos, cos, jnp.ones((seq, HEAD_DIM - ROT_DIM), F32)], axis=1)
    s1 = jnp.concatenate([-sin, jnp.zeros((seq, half), F32), zeros], axis=1)
    s2 = jnp.concatenate([jnp.zeros((seq, half), F32), sin, zeros], axis=1)
    return jnp.stack([c, s1, s2])


def _rope_tile(x, c, s1, s2):
    half = ROT_DIM // 2
    return x * c + pltpu.roll(x, HEAD_DIM - half, 1) * s1 + pltpu.roll(x, half, 1) * s2


def _proj_kernel(a_ref, w_ref, tab_ref, o_ref, *, tn, seg_cols, rope_mod, rope_cnt):
    j = pl.program_id(1)
    acc = jnp.dot(a_ref[...], w_ref[...].astype(BF16), preferred_element_type=F32)
    seg = (j * tn) // seg_cols
    do_rope = (seg % rope_mod) < rope_cnt

    @pl.when(do_rope)
    def _():
        c, s1, s2 = tab_ref[0], tab_ref[1], tab_ref[2]
        for h in range(tn // HEAD_DIM):
            sl = slice(h * HEAD_DIM, (h + 1) * HEAD_DIM)
            o_ref[:, sl] = _rope_tile(acc[:, sl], c, s1, s2).astype(o_ref.dtype)

    @pl.when(jnp.logical_not(do_rope))
    def _():
        o_ref[...] = acc.astype(o_ref.dtype)


def _proj(a, w, tabs, *, col0, n_out, pos_period, rope_mod, rope_cnt, tm=1024, tn=512):
    m, k = a.shape
    assert m % tm == 0 and n_out % tn == 0 and col0 % tn == 0 and pos_period % tm == 0
    jb = col0 // tn
    per = pos_period // tm
    kern = functools.partial(_proj_kernel, tn=tn, seg_cols=D_MODEL, rope_mod=rope_mod, rope_cnt=rope_cnt)
    return pl.pallas_call(
        kern,
        grid=(m // tm, n_out // tn),
        in_specs=[
            pl.BlockSpec((tm, k), lambda i, j: (i, 0)),
            pl.BlockSpec((k, tn), lambda i, j: (0, j + jb)),
            pl.BlockSpec((3, tm, HEAD_DIM), lambda i, j: (0, i % per, 0)),
        ],
        out_specs=pl.BlockSpec((tm, tn), lambda i, j: (i, j)),
        out_shape=jax.ShapeDtypeStruct((m, n_out), BF16),
        compiler_params=_cparams(2),
        name="proj_rope",
    )(a, w, tabs)


def _band_attn_kernel(*refs, has_prev, n_heads, blk):
    if has_prev:
        q_ref, kp_ref, kc_ref, vp_ref, vc_ref, o_ref, lse_ref = refs
    else:
        q_ref, kc_ref, vc_ref, o_ref, lse_ref = refs
    i = pl.program_id(1)
    nk = 2 * blk if has_prev else blk
    qi = lax.broadcasted_iota(jnp.int32, (blk, nk), 0)
    ki = lax.broadcasted_iota(jnp.int32, (blk, nk), 1) - (nk - blk)
    diff = qi - ki
    mask = (diff >= 0) & (diff <= blk)
    if has_prev:
        mask = mask & ((i > 0) | (ki >= 0))
    scale = HEAD_DIM ** -0.5
    lane = lax.broadcasted_iota(jnp.int32, (blk, n_heads), 1)
    lse_all = jnp.zeros((blk, n_heads), F32)
    for h in range(n_heads):
        sl = slice(h * HEAD_DIM, (h + 1) * HEAD_DIM)
        q = q_ref[:, sl]
        if has_prev:
            k = jnp.concatenate([kp_ref[:, sl], kc_ref[:, sl]], axis=0)
            v = jnp.concatenate([vp_ref[:, sl], vc_ref[:, sl]], axis=0)
        else:
            k, v = kc_ref[:, sl], vc_ref[:, sl]
        s = lax.dot_general(q, k, (((1,), (1,)), ((), ())), preferred_element_type=F32) * scale
        s = jnp.where(mask, s, NEG_INF)
        m = jnp.max(s, axis=1, keepdims=True)
        p = jnp.exp(s - m)
        den = jnp.sum(p, axis=1, keepdims=True)
        o = jnp.dot(p.astype(BF16), v, preferred_element_type=F32) / den
        o_ref[:, sl] = o.astype(o_ref.dtype)
        lse_all = jnp.where(lane == h, m + jnp.log(den), lse_all)
    lse_ref[...] = lse_all


def _band_attn(qkv, n_heads=A_HEADS, blk=128):
    n, L, c3 = qkv.shape
    c = c3 // 3
    assert c == n_heads * HEAD_DIM and L % blk == 0
    nb = L // blk
    has_prev = nb > 1
    spec = lambda col, prev: pl.BlockSpec(
        (None, blk, c), (lambda s, i: (s, jnp.maximum(i - 1, 0), col)) if prev else (lambda s, i: (s, i, col)))
    if has_prev:
        in_specs = [spec(0, False), spec(1, True), spec(1, False), spec(2, True), spec(2, False)]
        args = (qkv,) * 5
    else:
        in_specs = [spec(0, False), spec(1, False), spec(2, False)]
        args = (qkv,) * 3
    kern = functools.partial(_band_attn_kernel, has_prev=has_prev, n_heads=n_heads, blk=blk)
    return pl.pallas_call(
        kern,
        grid=(n, nb),
        in_specs=in_specs,
        out_specs=[pl.BlockSpec((None, blk, c), lambda s, i: (s, i, 0)),
                   pl.BlockSpec((None, blk, n_heads), lambda s, i: (s, i, 0))],
        out_shape=[jax.ShapeDtypeStruct((n, L, c), BF16), jax.ShapeDtypeStruct((n, L, n_heads), F32)],
        compiler_params=_cparams(2),
        name="band_attn",
    )(*args)


def _mix_kernel(o0, o1, o2, l0, l1, l2, out_ref, *, n_heads):
    ls = [l0[...], l1[...], l2[...]]
    mx = jnp.maximum(jnp.maximum(ls[0], ls[1]), ls[2])
    es = [jnp.exp(l - mx) for l in ls]
    tot = es[0] + es[1] + es[2]
    ws = [e / tot for e in es]
    os_ = [o0, o1, o2]
    for h in range(n_heads):
        sl = slice(h * HEAD_DIM, (h + 1) * HEAD_DIM)
        acc = ws[0][:, h:h + 1] * os_[0][:, sl].astype(F32)
        acc += ws[1][:, h:h + 1] * os_[1][:, sl].astype(F32)
        acc += ws[2][:, h:h + 1] * os_[2][:, sl].astype(F32)
        out_ref[:, sl] = acc.astype(out_ref.dtype)


def _mix(os_, ls, tm=256, n_heads=A_HEADS):
    m, c = os_[0].shape
    ospec = pl.BlockSpec((tm, c), lambda i: (i, 0))
    lspec = pl.BlockSpec((tm, n_heads), lambda i: (i, 0))
    return pl.pallas_call(
        functools.partial(_mix_kernel, n_heads=n_heads),
        grid=(m // tm,),
        in_specs=[ospec] * 3 + [lspec] * 3,
        out_specs=ospec,
        out_shape=jax.ShapeDtypeStruct((m, c), BF16),
        compiler_params=_cparams(1),
        name="group_mix",
    )(*os_, *ls)


def _layer_norm_rows(y, g, b):
    mu = jnp.mean(y, axis=1, keepdims=True)
    yc = y - mu
    var = jnp.mean(yc * yc, axis=1, keepdims=True)
    return yc * lax.rsqrt(var + LN_EPS) * g + b


def _oproj_ln_kernel(a_ref, w_ref, h_ref, g_ref, b_ref, of_ref, ob_ref):
    acc = jnp.dot(a_ref[...], w_ref[...], preferred_element_type=F32)
    y = _layer_norm_rows(DEEPNORM_ALPHA * h_ref[...] + acc, g_ref[...], b_ref[...])
    of_ref[...] = y
    ob_ref[...] = y.astype(ob_ref.dtype)


def _oproj_ln(a, w_bf, h, g, b, tm=512):
    m, k = a.shape
    d = w_bf.shape[1]
    row = lambda i: (i, 0)
    const = lambda i: (0, 0)
    return pl.pallas_call(
        _oproj_ln_kernel,
        grid=(m // tm,),
        in_specs=[pl.BlockSpec((tm, k), row), pl.BlockSpec((k, d), const), pl.BlockSpec((tm, d), row),
                  pl.BlockSpec((1, d), const), pl.BlockSpec((1, d), const)],
        out_specs=[pl.BlockSpec((tm, d), row), pl.BlockSpec((tm, d), row)],
        out_shape=[jax.ShapeDtypeStruct((m, d), F32), jax.ShapeDtypeStruct((m, d), BF16)],
        compiler_params=_cparams(1),
        name="oproj_ln",
    )(a, w_bf, h, g.reshape(1, d), b.reshape(1, d))


def _moba_kernel(q_ref, k_ref, v_ref, et_ref, o_ref, *, heads, seq, blk):
    c = pl.program_id(2)
    nblk = seq // blk
    hw = heads * HEAD_DIM
    scale = HEAD_DIM ** -0.5
    ar = lax.broadcasted_iota(jnp.int32, (16, seq), 0)
    ac = lax.broadcasted_iota(jnp.int32, (16, seq), 1)
    avg = jnp.where(ac // blk == ar, 1.0 / blk, 0.0).astype(BF16)
    kmean = jnp.dot(avg, k_ref[...], preferred_element_type=F32)[0:nblk]
    kmt = jnp.concatenate([kmean] * (LANES // nblk), axis=0)
    kr = lax.broadcasted_iota(jnp.int32, (LANES, hw), 0)
    kc = lax.broadcasted_iota(jnp.int32, (LANES, hw), 1)
    km = jnp.where(kr // nblk == kc // HEAD_DIM, kmt, 0.0).astype(BF16)
    gate = lax.dot_general(q_ref[...], km, (((1,), (1,)), ((), ())), preferred_element_type=F32)
    lane = lax.broadcasted_iota(jnp.int32, (blk, LANES), 1)
    n = lane % nblk
    valid = (n < c) & (lane < heads * nblk)
    g = jnp.where(valid, gate, NEG_INF)
    cnt = jnp.zeros((blk, LANES), jnp.int32)
    for sh in range(1, nblk):
        lo = pltpu.roll(g, sh, 1)
        cnt = cnt + jnp.where((n >= sh) & (lo >= g), 1, 0)
        hi = pltpu.roll(g, LANES - sh, 1)
        cnt = cnt + jnp.where((n + sh < nblk) & (hi > g), 1, 0)
    sel = (cnt < MOBA_TOPK) & valid
    bias = jnp.where(sel | (n == c), 0.0, NEG_INF).astype(F32)

    for nb in range(2, nblk + 1, 2):
        @pl.when(2 * (c // 2 + 1) == nb)
        def _():
            kw = nb * blk
            qpos = c * blk + lax.broadcasted_iota(jnp.int32, (blk, kw), 0)
            kpos = lax.broadcasted_iota(jnp.int32, (blk, kw), 1)
            causal = kpos <= qpos
            et = et_ref[0:kw, :]
            for h in range(heads):
                sl = slice(h * HEAD_DIM, (h + 1) * HEAD_DIM)
                bias_h = bias if h == 0 else pltpu.roll(bias, LANES - h * nblk, 1)
                q_aug = jnp.concatenate([q_ref[:, sl], bias_h.astype(BF16)], axis=1)
                k_aug = jnp.concatenate([k_ref[0:kw, sl], et], axis=1)
                s = lax.dot_general(q_aug, k_aug, (((1,), (1,)), ((), ())), preferred_element_type=F32) * scale
                s = jnp.where(causal, s, NEG_INF)
                m = jnp.max(s, axis=1, keepdims=True)
                p = jnp.exp(s - m)
                den = jnp.sum(p, axis=1, keepdims=True)
                o = jnp.dot(p.astype(BF16), v_ref[0:kw, sl], preferred_element_type=F32) / den
                o_ref[:, sl] = o.astype(o_ref.dtype)


def _moba(q, kv, heads_per_step=4, blk=MOBA_BLOCK):
    b, s, c = q.shape
    n_heads = c // HEAD_DIM
    nblk = s // blk
    assert nblk % 2 == 0 and LANES % nblk == 0 and heads_per_step * nblk <= LANES
    hw = heads_per_step * HEAD_DIM
    ng = n_heads // heads_per_step
    et = (jnp.arange(s, dtype=jnp.int32)[:, None] // blk == jnp.arange(LANES, dtype=jnp.int32)[None, :]).astype(BF16)
    kern = functools.partial(_moba_kernel, heads=heads_per_step, seq=s, blk=blk)
    return pl.pallas_call(
        kern,
        grid=(b, ng, s // blk),
        in_specs=[pl.BlockSpec((None, blk, hw), lambda bi, g, t: (bi, t, g)),
                  pl.BlockSpec((None, s, hw), lambda bi, g, t: (bi, 0, g)),
                  pl.BlockSpec((None, s, hw), lambda bi, g, t: (bi, 0, g + ng)),
                  pl.BlockSpec((s, LANES), lambda bi, g, t: (0, 0))],
        out_specs=pl.BlockSpec((None, blk, hw), lambda bi, g, t: (bi, t, g)),
        out_shape=jax.ShapeDtypeStruct((b, s, c), BF16),
        compiler_params=_cparams(3),
        name="moba_attn",
    )(q, kv, kv, et)


def _router_kernel(h_ref, wr_ref, br_ref, idx_ref, w_ref, rank_ref, cnt_ref, *, tm):
    t = pl.program_id(0)

    @pl.when(t == 0)
    def _():
        cnt_ref[...] = jnp.zeros_like(cnt_ref)

    logits = lax.dot_general(wr_ref[...], h_ref[...], (((1,), (1,)), ((), ())),
                             preferred_element_type=F32) + br_ref[...]
    row = lax.broadcasted_iota(jnp.int32, (N_EXPERTS, tm), 0)
    rem = logits
    vals, idxs, hots = [], [], []
    for _ in range(TOP_K):
        mx = jnp.max(rem, axis=0, keepdims=True)
        ix = jnp.min(jnp.where(rem == mx, row, N_EXPERTS), axis=0, keepdims=True)
        hot = row == ix
        vals.append(mx)
        idxs.append(ix)
        hots.append(hot)
        rem = jnp.where(hot, -jnp.inf, rem)
    es = [jnp.exp(v - vals[0]) for v in vals]
    tot = es[0] + es[1] + es[2] + es[3]
    sel = jnp.zeros((N_EXPERTS, tm), F32)
    for hot in hots:
        sel = sel + hot.astype(F32)
    ri = lax.broadcasted_iota(jnp.int32, (tm, tm), 0)
    ci = lax.broadcasted_iota(jnp.int32, (tm, tm), 1)
    upper = jnp.where(ri <= ci, 1.0, 0.0).astype(BF16)
    incl = jnp.dot(sel.astype(BF16), upper, preferred_element_type=F32)
    base = cnt_ref[:, 0:1]
    rank_e = base + incl - sel
    ranks = [jnp.sum(jnp.where(hot, rank_e, 0.0), axis=0, keepdims=True) for hot in hots]
    idx_ref[...] = jnp.concatenate(idxs, axis=0)
    w_ref[...] = jnp.concatenate([e / tot for e in es], axis=0)
    rank_ref[...] = jnp.concatenate(ranks, axis=0).astype(jnp.int32)
    cnt_ref[...] = jnp.broadcast_to(base + incl[:, tm - 1:tm], cnt_ref.shape)


def _router(h_bf, w_router, b_router, tm=512):
    n_tok, d = h_bf.shape
    wr_t = w_router.T.astype(BF16)
    br = b_router.reshape(N_EXPERTS, 1).astype(F32)
    tok = lambda i: (0, i)
    const = lambda i: (0, 0)
    return pl.pallas_call(
        functools.partial(_router_kernel, tm=tm),
        grid=(n_tok // tm,),
        in_specs=[pl.BlockSpec((tm, d), lambda i: (i, 0)), pl.BlockSpec((N_EXPERTS, d), const),
                  pl.BlockSpec((N_EXPERTS, 1), const)],
        out_specs=[pl.BlockSpec((TOP_K, tm), tok), pl.BlockSpec((TOP_K, tm), tok), pl.BlockSpec((TOP_K, tm), tok),
                   pl.BlockSpec((N_EXPERTS, LANES), const)],
        out_shape=[jax.ShapeDtypeStruct((TOP_K, n_tok), jnp.int32), jax.ShapeDtypeStruct((TOP_K, n_tok), F32),
                   jax.ShapeDtypeStruct((TOP_K, n_tok), jnp.int32), jax.ShapeDtypeStruct((N_EXPERTS, LANES), F32)],
        compiler_params=_cparams(1),
        name="moe_router",
    )(h_bf, wr_t, br)


SUB = 256
SUPER = 1024
SUBS_PER_SUPER = SUPER // SUB


def _moe_tables(counts, n_slots):
    n_super_max = N_EXPERTS + n_slots // SUPER
    n_sub_e = (counts + SUB - 1) // SUB
    xs_start = SUB * (jnp.cumsum(n_sub_e) - n_sub_e)
    n_sb_e = (counts + SUPER - 1) // SUPER
    sb_end = jnp.cumsum(n_sb_e)
    sb_start = sb_end - n_sb_e
    total = sb_end[-1]
    g = jnp.arange(n_super_max, dtype=jnp.int32)
    gc = jnp.minimum(g, total - 1)
    e_of = jnp.minimum(jnp.searchsorted(sb_end, gc, side="right"), N_EXPERTS - 1).astype(jnp.int32)
    j_in = gc - sb_start[e_of]
    blk0 = xs_start[e_of] // SUB + SUBS_PER_SUPER * j_in
    nsub = jnp.clip(n_sub_e[e_of] - SUBS_PER_SUPER * j_in, 0, SUBS_PER_SUPER)
    nsub_active = jnp.where(g < total, nsub, 0)
    ys_start = SUPER * sb_start
    xs_tail = xs_start + SUB * jnp.maximum(n_sub_e - 1, 0)
    return (xs_start.astype(jnp.int32), ys_start.astype(jnp.int32), e_of, blk0.astype(jnp.int32),
            nsub.astype(jnp.int32), nsub_active.astype(jnp.int32), gc.astype(jnp.int32), xs_tail.astype(jnp.int32))


def _dispatch_kernel(start_ref, tail_ref, used_ref, idx_ref, rank_ref, h_ref, xs_hbm, zbuf, sem, zsem, *, tm):
    def tail_copy(e):
        return pltpu.make_async_copy(zbuf, xs_hbm.at[pl.ds(pl.multiple_of(tail_ref[e], SUB), SUB)], zsem)

    @pl.when(pl.program_id(0) == 0)
    def _():
        zbuf[...] = jnp.zeros_like(zbuf)
        for e in range(N_EXPERTS):
            @pl.when(used_ref[e] > 0)
            def _():
                tail_copy(e).start()
        for e in range(N_EXPERTS):
            @pl.when(used_ref[e] > 0)
            def _():
                tail_copy(e).wait()

    def row_copy(t, k):
        dst = start_ref[idx_ref[k, t]] + rank_ref[k, t]
        return pltpu.make_async_copy(h_ref.at[pl.ds(t, 1)], xs_hbm.at[pl.ds(dst, 1)], sem)

    def issue(t, carry):
        for k in range(TOP_K):
            row_copy(t, k).start()
        return carry

    def drain(t, carry):
        for k in range(TOP_K):
            row_copy(t, k).wait()
        return carry

    lax.fori_loop(0, tm, issue, 0)
    lax.fori_loop(0, tm, drain, 0)


def _dispatch(h, top_idx, rank, xs_start, xs_tail, counts, n_rows, tm=256):
    n_tok, d = h.shape
    smem_blk = lambda: pl.BlockSpec((TOP_K, tm), lambda i, *_: (0, i), memory_space=pltpu.SMEM)
    grid_spec = pltpu.PrefetchScalarGridSpec(
        num_scalar_prefetch=3,
        grid=(n_tok // tm,),
        in_specs=[smem_blk(), smem_blk(), pl.BlockSpec((tm, d), lambda i, *_: (i, 0))],
        out_specs=pl.BlockSpec(memory_space=pl.ANY),
        scratch_shapes=[pltpu.VMEM((SUB, d), h.dtype), pltpu.SemaphoreType.DMA, pltpu.SemaphoreType.DMA],
    )
    return pl.pallas_call(
        functools.partial(_dispatch_kernel, tm=tm),
        grid_spec=grid_spec,
        out_shape=jax.ShapeDtypeStruct((n_rows, d), h.dtype),
        compiler_params=_cparams(1),
        name="moe_dispatch",
    )(xs_start, xs_tail, counts, top_idx, rank, h)


def _ffn_kernel(e_ref, blk0_ref, nsub_ref, act_ref, out_ref_idx, x0, x1, x2, x3, wg_ref, wu_ref, bg_ref, bu_ref,
                wd_ref, bd_ref, y_ref, xb, *, tf):
    del e_ref, blk0_ref, nsub_ref, out_ref_idx
    g = pl.program_id(0)
    j = pl.program_id(1)
    nact = act_ref[g]

    @pl.when(j == 0)
    def _():
        for s, x_ref in enumerate((x0, x1, x2, x3)):
            @pl.when(s < nact)
            def _():
                xb[s * SUB:(s + 1) * SUB, :] = x_ref[...].astype(BF16)

    for n in range(1, SUBS_PER_SUPER + 1):
        @pl.when(nact == n)
        def _():
            m = n * SUB
            x = xb[0:m, :]
            gate = jnp.dot(x, wg_ref[...].astype(BF16), preferred_element_type=F32) + bg_ref[...]
            lin = jnp.dot(x, wu_ref[...].astype(BF16), preferred_element_type=F32) + bu_ref[...]
            gate = jnp.minimum(gate, SWIGLU_LIMIT)
            lin = jnp.clip(lin, -SWIGLU_LIMIT, SWIGLU_LIMIT)
            hid = (lin + 1.0) * gate * jax.nn.sigmoid(SWIGLU_ALPHA * gate)

            @pl.when(j == 0)
            def _():
                y_ref[0:m, :] = jnp.broadcast_to(bd_ref[...], (m, y_ref.shape[1]))

            y_ref[0:m, :] += jnp.dot(hid.astype(BF16), wd_ref[...].astype(BF16), preferred_element_type=F32)


def _expert_ffn(xs, tables, layer, w_gu, b_gu, w_down, b_down, n_super_max, tf=256):
    e_of, blk0, nsub, nsub_active, g_out = tables[2:7]
    n_rows, d = xs.shape
    n_we = w_gu.shape[0] * N_EXPERTS
    e_of = e_of + layer * N_EXPERTS
    w_gu = w_gu.reshape(n_we, d, 2 * D_FF)
    w_down = w_down.reshape(n_we, D_FF, d)
    b_gu = b_gu.reshape(n_we, 1, 2 * D_FF)
    b_down = b_down.reshape(n_we, 1, d)
    n_ff = D_FF // tf
    last = n_ff - 1

    def x_spec(s):
        return pl.BlockSpec(
            (SUB, d), lambda g, j, e, b0, ns, na, go: (b0[g] + jnp.minimum(s, jnp.maximum(ns[g], 1) - 1), 0))

    def jj(j, na, g):
        return jnp.where(na[g] > 0, j, last)

    in_specs = [x_spec(s) for s in range(SUBS_PER_SUPER)] + [
        pl.BlockSpec((None, d, tf), lambda g, j, e, b0, ns, na, go: (e[g], 0, jj(j, na, g))),
        pl.BlockSpec((None, d, tf), lambda g, j, e, b0, ns, na, go: (e[g], 0, n_ff + jj(j, na, g))),
        pl.BlockSpec((None, 1, tf), lambda g, j, e, b0, ns, na, go: (e[g], 0, jj(j, na, g))),
        pl.BlockSpec((None, 1, tf), lambda g, j, e, b0, ns, na, go: (e[g], 0, n_ff + jj(j, na, g))),
        pl.BlockSpec((None, tf, d), lambda g, j, e, b0, ns, na, go: (e[g], jj(j, na, g), 0)),
        pl.BlockSpec((None, 1, d), lambda g, j, e, b0, ns, na, go: (e[g], 0, 0)),
    ]
    grid_spec = pltpu.PrefetchScalarGridSpec(
        num_scalar_prefetch=5,
        grid=(n_super_max, n_ff),
        in_specs=in_specs,
        out_specs=pl.BlockSpec((SUPER, d), lambda g, j, e, b0, ns, na, go: (go[g], 0)),
        scratch_shapes=[pltpu.VMEM((SUPER, d), BF16)],
    )
    return pl.pallas_call(
        functools.partial(_ffn_kernel, tf=tf),
        grid_spec=grid_spec,
        out_shape=jax.ShapeDtypeStruct((n_super_max * SUPER, d), F32),
        compiler_params=_cparams(2),
        name="moe_ffn",
    )(e_of, blk0, nsub, nsub_active, g_out, xs, xs, xs, xs, w_gu, w_gu, b_gu, b_gu, w_down, b_down)


def _combine_kernel(start_ref, idx_ref, rank_ref, w_ref, h_ref, g_ref, b_ref, ys_hbm, of_ref, ob_ref, ybuf, sem,
                    *, tm):
    def row_copy(t, k):
        src = start_ref[idx_ref[k, t]] + rank_ref[k, t]
        return pltpu.make_async_copy(ys_hbm.at[pl.ds(src, 1)], ybuf.at[k, pl.ds(t, 1)], sem)

    def issue(t, carry):
        for k in range(TOP_K):
            row_copy(t, k).start()
        return carry

    def drain(t, carry):
        for k in range(TOP_K):
            row_copy(t, k).wait()
        return carry

    lax.fori_loop(0, tm, issue, 0)
    lax.fori_loop(0, tm, drain, 0)
    w = w_ref[...]
    ffn = w[:, 0:1] * ybuf[0]
    for k in range(1, TOP_K):
        ffn = ffn + w[:, k:k + 1] * ybuf[k]
    y = _layer_norm_rows(DEEPNORM_ALPHA * h_ref[...] + ffn, g_ref[...], b_ref[...])
    of_ref[...] = y
    ob_ref[...] = y.astype(ob_ref.dtype)


def _combine_ln(ys, top_idx, rank, w_t, ys_start, h, g, b, tm=128):
    n_tok, d = h.shape
    smem_blk = lambda: pl.BlockSpec((TOP_K, tm), lambda i, s: (0, i), memory_space=pltpu.SMEM)
    row = lambda i, s: (i, 0)
    const = lambda i, s: (0, 0)
    grid_spec = pltpu.PrefetchScalarGridSpec(
        num_scalar_prefetch=1,
        grid=(n_tok // tm,),
        in_specs=[smem_blk(), smem_blk(), pl.BlockSpec((tm, TOP_K), row), pl.BlockSpec((tm, d), row),
                  pl.BlockSpec((1, d), const), pl.BlockSpec((1, d), const), pl.BlockSpec(memory_space=pl.ANY)],
        out_specs=[pl.BlockSpec((tm, d), row), pl.BlockSpec((tm, d), row)],
        scratch_shapes=[pltpu.VMEM((TOP_K, tm, d), F32), pltpu.SemaphoreType.DMA],
    )
    return pl.pallas_call(
        functools.partial(_combine_kernel, tm=tm),
        grid_spec=grid_spec,
        out_shape=[jax.ShapeDtypeStruct((n_tok, d), F32), jax.ShapeDtypeStruct((n_tok, d), BF16)],
        compiler_params=_cparams(1),
        name="moe_combine_ln",
    )(ys_start, top_idx, rank, w_t, h, g.reshape(1, d), b.reshape(1, d), ys)


def _moe_ln(h, h_bf, layer, w_router, b_router, w_gu, b_gu, w_down, b_down, ln_g, ln_b):
    n_tok, d = h.shape
    n_slots = n_tok * TOP_K
    top_idx, top_w, rank, cnt = _router(h_bf, w_router, b_router)
    counts = cnt[:, 0].astype(jnp.int32)
    tables = _moe_tables(counts, n_slots)
    xs_start, ys_start = tables[0], tables[1]
    n_super_max = N_EXPERTS + n_slots // SUPER
    n_rows = n_slots + N_EXPERTS * SUB
    xs = _dispatch(h, top_idx, rank, xs_start, tables[7], counts, n_rows)
    ys = _expert_ffn(xs, tables, layer, w_gu, b_gu, w_down, b_down, n_super_max)
    return _combine_ln(ys, top_idx, rank, top_w.T, ys_start, h, ln_g, ln_b)


def _dilated_mix(x3, w_qkv, tabs):
    b, s, d = x3.shape
    n_tok = b * s
    width = A_HEADS * HEAD_DIM
    outs, lses = [], []
    for g, (window, dil) in enumerate(A_PATTERNS):
        assert window // dil == 128
        L = s // dil
        xp = x3.reshape(b, L, dil, d).transpose(0, 2, 1, 3).reshape(n_tok, d).astype(BF16)
        tabs_p = tabs.reshape(3, L, dil, HEAD_DIM).transpose(0, 2, 1, 3).reshape(3, s, HEAD_DIM)
        qkv = _proj(xp, w_qkv, tabs_p, col0=g * 3 * width, n_out=3 * width, pos_period=s, rope_mod=3, rope_cnt=2)
        o, lse = _band_attn(qkv.reshape(b * dil, L, 3 * width))
        outs.append(o.reshape(b, dil, L, width).transpose(0, 2, 1, 3).reshape(n_tok, width))
        lses.append(lse.reshape(b, dil, L, A_HEADS).transpose(0, 2, 1, 3).reshape(n_tok, A_HEADS))
    return _mix(outs, lses)


def kernel(x, a_w_qkv, a_w_o, kv_w, b_w_q, b_w_o, router_w, router_b, moe_w_gate_up, moe_b_gate_up, moe_w_down,
           moe_b_down, ln1_g, ln1_b, ln2_g, ln2_b):
    b, s, d = x.shape
    n_tok = b * s
    tabs = _rope_tables(s)
    h = x.reshape(n_tok, d)
    h_bf = h.astype(BF16)
    kv = None
    for layer in range(DEPTH):
        if layer < N_A_LAYERS:
            mix = _dilated_mix(h.reshape(b, s, d), a_w_qkv[layer], tabs)
            w_o = a_w_o[layer]
        else:
            j = layer - N_A_LAYERS
            if layer == N_A_LAYERS:
                kv = _proj(h_bf, kv_w, tabs, col0=0, n_out=2 * B_HEADS * HEAD_DIM, pos_period=s, rope_mod=2,
                           rope_cnt=1)
            q = _proj(h_bf, b_w_q[j], tabs, col0=0, n_out=B_HEADS * HEAD_DIM, pos_period=s, rope_mod=1, rope_cnt=1)
            mix = _moba(q.reshape(b, s, -1), kv.reshape(b, s, -1)).reshape(n_tok, -1)
            w_o = b_w_o[j]
        h, h_bf = _oproj_ln(mix, w_o.astype(BF16), h, ln1_g[layer], ln1_b[layer])
        h, h_bf = _moe_ln(h, h_bf, layer, router_w[layer], router_b[layer], moe_w_gate_up, moe_b_gate_up,
                          moe_w_down, moe_b_down, ln2_g[layer], ln2_b[layer])
    return h.reshape(b, s, d)
```

```python
import functools

import jax
import jax.numpy as jnp
from jax import lax
from jax.experimental import pallas as pl
from jax.experimental.pallas import tpu as pltpu

D_MODEL = 2048
SEQ = 2048
DEPTH = 2
HEAD_DIM = 128
ROT_DIM = HEAD_DIM // 4
ROPE_THETA = 500000.0
A_PATTERNS = ((128, 1), (512, 4), (2048, 16))
A_GROUPS = len(A_PATTERNS)
A_HEADS = 16
B_HEADS = 16
MOBA_BLOCK = 256
MOBA_TOPK = 3
N_EXPERTS = 32
TOP_K = 4
D_FF = 2048
SWIGLU_LIMIT = 7.0
SWIGLU_ALPHA = 1.702
N_A_LAYERS = DEPTH // 2
DEEPNORM_ALPHA = (2 * DEPTH) ** 0.25
LN_EPS = 1e-5
NEG_INF = -1e30

LANES = 128
V7X_VMEM_BYTES = 64 * 1024 * 1024
VMEM_LIMIT = 56 * 1024 * 1024

BF16 = jnp.bfloat16
F32 = jnp.float32


def _cparams(n_axes):
    return pltpu.CompilerParams(dimension_semantics=("arbitrary",) * n_axes, vmem_limit_bytes=VMEM_LIMIT)


def _rope_tables(seq):
    half = ROT_DIM // 2
    inv = ROPE_THETA ** (-jnp.arange(0, ROT_DIM, 2, dtype=F32) / ROT_DIM)
    ang = jnp.arange(seq, dtype=F32)[:, None] * inv[None, :]
    cos, sin = jnp.cos(ang), jnp.sin(ang)
    zeros = jnp.zeros((seq, HEAD_DIM - ROT_DIM), F32)
    c = jnp.concatenate([cos, cos, jnp.ones((seq, HEAD_DIM - ROT_DIM), F32)], axis=1)
    s1 = jnp.concatenate([-sin, jnp.zeros((seq, half), F32), zeros], axis=1)
    s2 = jnp.concatenate([jnp.zeros((seq, half), F32), sin, zeros], axis=1)
    return jnp.stack([c, s1, s2])


def _rope_tile(x, c, s1, s2):
    half = ROT_DIM // 2
    return x * c + pltpu.roll(x, HEAD_DIM - half, 1) * s1 + pltpu.roll(x, half, 1) * s2


def _proj_kernel(a_ref, w_ref, tab_ref, o_ref, acc_s, *, tn):
    @pl.when(pl.program_id(0) == 0)
    def _():
        acc_s[...] = jnp.zeros_like(acc_s)

    prev = acc_s[...]
    c, s1, s2 = tab_ref[0], tab_ref[1], tab_ref[2]
    for h in range(tn // HEAD_DIM):
        sl = slice(h * HEAD_DIM, (h + 1) * HEAD_DIM)
        o_ref[:, sl] = _rope_tile(prev[:, sl], c, s1, s2).astype(o_ref.dtype)
    acc_s[...] = jnp.dot(a_ref[...], w_ref[...].astype(BF16), preferred_element_type=F32)


def _proj(a, w, tabs, *, col0, n_out, pos_period, rope_mod, rope_cnt, tm=1024, tn=512):
    m, k = a.shape
    assert m % tm == 0 and n_out % tn == 0 and col0 % tn == 0 and pos_period % tm == 0 and D_MODEL % tn == 0
    jb = col0 // tn
    per = pos_period // tm
    ident = jnp.stack([jnp.ones_like(tabs[0]), jnp.zeros_like(tabs[0]), jnp.zeros_like(tabs[0])])
    tabs2 = jnp.stack([tabs, ident])

    nj = n_out // tn
    n_tiles = (m // tm) * nj

    def plain(j):
        return jnp.where(((j * tn) // D_MODEL) % rope_mod < rope_cnt, 0, 1)

    cur = lambda t: jnp.minimum(t, n_tiles - 1)
    fin = lambda t: jnp.maximum(t - 1, 0)
    return pl.pallas_call(
        functools.partial(_proj_kernel, tn=tn),
        grid=(n_tiles + 1,),
        in_specs=[
            pl.BlockSpec((tm, k), lambda t: (cur(t) // nj, 0)),
            pl.BlockSpec((k, tn), lambda t: (0, cur(t) % nj + jb)),
            pl.BlockSpec((None, 3, tm, HEAD_DIM), lambda t: (plain(fin(t) % nj), 0, (fin(t) // nj) % per, 0)),
        ],
        out_specs=pl.BlockSpec((tm, tn), lambda t: (fin(t) // nj, fin(t) % nj)),
        out_shape=jax.ShapeDtypeStruct((m, n_out), BF16),
        scratch_shapes=[pltpu.VMEM((tm, tn), F32)],
        compiler_params=_cparams(1),
        name="proj_rope",
    )(a, w, tabs2)


def _band_attn_kernel(*refs, has_prev, n_heads, blk):
    if has_prev:
        q_ref, kp_ref, kc_ref, vp_ref, vc_ref, o_ref, lse_ref = refs
    else:
        q_ref, kc_ref, vc_ref, o_ref, lse_ref = refs
    i = pl.program_id(1)
    nk = 2 * blk if has_prev else blk
    qi = lax.broadcasted_iota(jnp.int32, (blk, nk), 0)
    ki = lax.broadcasted_iota(jnp.int32, (blk, nk), 1) - (nk - blk)
    diff = qi - ki
    mask = (diff >= 0) & (diff <= blk)
    if has_prev:
        mask = mask & ((i > 0) | (ki >= 0))
    scale = HEAD_DIM ** -0.5
    lane = lax.broadcasted_iota(jnp.int32, (blk, n_heads), 1)
    lse_all = jnp.zeros((blk, n_heads), F32)
    for h in range(n_heads):
        sl = slice(h * HEAD_DIM, (h + 1) * HEAD_DIM)
        q = q_ref[:, sl]
        if has_prev:
            k = jnp.concatenate([kp_ref[:, sl], kc_ref[:, sl]], axis=0)
            v = jnp.concatenate([vp_ref[:, sl], vc_ref[:, sl]], axis=0)
        else:
            k, v = kc_ref[:, sl], vc_ref[:, sl]
        s = lax.dot_general(q, k, (((1,), (1,)), ((), ())), preferred_element_type=F32) * scale
        s = jnp.where(mask, s, NEG_INF)
        m = jnp.max(s, axis=1, keepdims=True)
        p = jnp.exp(s - m)
        den = jnp.sum(p, axis=1, keepdims=True)
        o = jnp.dot(p.astype(BF16), v, preferred_element_type=F32) / den
        o_ref[:, sl] = o.astype(o_ref.dtype)
        lse_all = jnp.where(lane == h, m + jnp.log(den), lse_all)
    lse_ref[...] = lse_all


def _band_attn(qkv, n_heads=A_HEADS, blk=128):
    n, L, c3 = qkv.shape
    c = c3 // 3
    assert c == n_heads * HEAD_DIM and L % blk == 0
    nb = L // blk
    has_prev = nb > 1
    spec = lambda col, prev: pl.BlockSpec(
        (None, blk, c), (lambda s, i: (s, jnp.maximum(i - 1, 0), col)) if prev else (lambda s, i: (s, i, col)))
    if has_prev:
        in_specs = [spec(0, False), spec(1, True), spec(1, False), spec(2, True), spec(2, False)]
        args = (qkv,) * 5
    else:
        in_specs = [spec(0, False), spec(1, False), spec(2, False)]
        args = (qkv,) * 3
    kern = functools.partial(_band_attn_kernel, has_prev=has_prev, n_heads=n_heads, blk=blk)
    return pl.pallas_call(
        kern,
        grid=(n, nb),
        in_specs=in_specs,
        out_specs=[pl.BlockSpec((None, blk, c), lambda s, i: (s, i, 0)),
                   pl.BlockSpec((None, blk, n_heads), lambda s, i: (s, i, 0))],
        out_shape=[jax.ShapeDtypeStruct((n, L, c), BF16), jax.ShapeDtypeStruct((n, L, n_heads), F32)],
        compiler_params=_cparams(2),
        name="band_attn",
    )(*args)


def _mix_kernel(o0, o1, o2, l0, l1, l2, out_ref, *, n_heads):
    ls = [l0[...], l1[...], l2[...]]
    mx = jnp.maximum(jnp.maximum(ls[0], ls[1]), ls[2])
    es = [jnp.exp(l - mx) for l in ls]
    tot = es[0] + es[1] + es[2]
    ws = [e / tot for e in es]
    os_ = [o0, o1, o2]
    for h in range(n_heads):
        sl = slice(h * HEAD_DIM, (h + 1) * HEAD_DIM)
        acc = ws[0][:, h:h + 1] * os_[0][:, sl].astype(F32)
        acc += ws[1][:, h:h + 1] * os_[1][:, sl].astype(F32)
        acc += ws[2][:, h:h + 1] * os_[2][:, sl].astype(F32)
        out_ref[:, sl] = acc.astype(out_ref.dtype)


def _mix(os_, ls, tm=256, n_heads=A_HEADS):
    m, c = os_[0].shape
    ospec = pl.BlockSpec((tm, c), lambda i: (i, 0))
    lspec = pl.BlockSpec((tm, n_heads), lambda i: (i, 0))
    return pl.pallas_call(
        functools.partial(_mix_kernel, n_heads=n_heads),
        grid=(m // tm,),
        in_specs=[ospec] * 3 + [lspec] * 3,
        out_specs=ospec,
        out_shape=jax.ShapeDtypeStruct((m, c), BF16),
        compiler_params=_cparams(1),
        name="group_mix",
    )(*os_, *ls)


def _layer_norm_rows(y, g, b):
    mu = jnp.mean(y, axis=1, keepdims=True)
    yc = y - mu
    var = jnp.mean(yc * yc, axis=1, keepdims=True)
    return yc * lax.rsqrt(var + LN_EPS) * g + b


def _pack_pairs(y):
    half = y.shape[1] // 2
    lo = lax.bitcast_convert_type(y[:, :half].astype(BF16).astype(F32), jnp.uint32)
    hi = lax.bitcast_convert_type(y[:, half:].astype(BF16).astype(F32), jnp.uint32)
    return hi | (lo >> 16)


def _unpack_pairs(u):
    lo = lax.bitcast_convert_type(u << 16, F32).astype(BF16)
    hi = lax.bitcast_convert_type(u & jnp.uint32(0xFFFF0000), F32).astype(BF16)
    return lo, hi


def _oproj_ln_kernel(a_ref, w_ref, h_ref, g_ref, b_ref, of_ref, ob_ref, op_ref):
    acc = jnp.dot(a_ref[...], w_ref[...], preferred_element_type=F32)
    y = _layer_norm_rows(DEEPNORM_ALPHA * h_ref[...] + acc, g_ref[...], b_ref[...])
    of_ref[...] = y
    ob_ref[...] = y.astype(ob_ref.dtype)
    op_ref[...] = _pack_pairs(y)


def _oproj_ln(a, w_bf, h, g, b, tm=512):
    m, k = a.shape
    d = w_bf.shape[1]
    row = lambda i: (i, 0)
    const = lambda i: (0, 0)
    return pl.pallas_call(
        _oproj_ln_kernel,
        grid=(m // tm,),
        in_specs=[pl.BlockSpec((tm, k), row), pl.BlockSpec((k, d), const), pl.BlockSpec((tm, d), row),
                  pl.BlockSpec((1, d), const), pl.BlockSpec((1, d), const)],
        out_specs=[pl.BlockSpec((tm, d), row), pl.BlockSpec((tm, d), row), pl.BlockSpec((tm, d // 2), row)],
        out_shape=[jax.ShapeDtypeStruct((m, d), F32), jax.ShapeDtypeStruct((m, d), BF16),
                   jax.ShapeDtypeStruct((m, d // 2), jnp.uint32)],
        compiler_params=_cparams(1),
        name="oproj_ln",
    )(a, w_bf, h, g.reshape(1, d), b.reshape(1, d))


def _moba_kernel(q_ref, k_ref, v_ref, et_ref, o_ref, *, heads, seq, blk):
    c = pl.program_id(2)
    nblk = seq // blk
    hw = heads * HEAD_DIM
    scale = HEAD_DIM ** -0.5
    ar = lax.broadcasted_iota(jnp.int32, (16, seq), 0)
    ac = lax.broadcasted_iota(jnp.int32, (16, seq), 1)
    avg = jnp.where(ac // blk == ar, 1.0 / blk, 0.0).astype(BF16)
    kmean = jnp.dot(avg, k_ref[...], preferred_element_type=F32)[0:nblk]
    kmt = jnp.concatenate([kmean] * (LANES // nblk), axis=0)
    kr = lax.broadcasted_iota(jnp.int32, (LANES, hw), 0)
    kc = lax.broadcasted_iota(jnp.int32, (LANES, hw), 1)
    km = jnp.where(kr // nblk == kc // HEAD_DIM, kmt, 0.0).astype(BF16)
    gate = lax.dot_general(q_ref[...], km, (((1,), (1,)), ((), ())), preferred_element_type=F32)
    lane = lax.broadcasted_iota(jnp.int32, (blk, LANES), 1)
    n = lane % nblk
    valid = (n < c) & (lane < heads * nblk)
    g = jnp.where(valid, gate, NEG_INF)
    cnt = jnp.zeros((blk, LANES), jnp.int32)
    for sh in range(1, nblk):
        lo = pltpu.roll(g, sh, 1)
        cnt = cnt + jnp.where((n >= sh) & (lo >= g), 1, 0)
        hi = pltpu.roll(g, LANES - sh, 1)
        cnt = cnt + jnp.where((n + sh < nblk) & (hi > g), 1, 0)
    sel = (cnt < MOBA_TOPK) & valid
    bias = jnp.where(sel | (n == c), 0.0, NEG_INF).astype(F32)

    for nb in range(2, nblk + 1, 2):
        @pl.when(2 * (c // 2 + 1) == nb)
        def _():
            kw = nb * blk
            qpos = c * blk + lax.broadcasted_iota(jnp.int32, (blk, kw), 0)
            kpos = lax.broadcasted_iota(jnp.int32, (blk, kw), 1)
            causal = kpos <= qpos
            et = et_ref[0:kw, :]
            for h in range(heads):
                sl = slice(h * HEAD_DIM, (h + 1) * HEAD_DIM)
                bias_h = bias if h == 0 else pltpu.roll(bias, LANES - h * nblk, 1)
                q_aug = jnp.concatenate([q_ref[:, sl], bias_h.astype(BF16)], axis=1)
                k_aug = jnp.concatenate([k_ref[0:kw, sl], et], axis=1)
                s = lax.dot_general(q_aug, k_aug, (((1,), (1,)), ((), ())), preferred_element_type=F32) * scale
                s = jnp.where(causal, s, NEG_INF)
                m = jnp.max(s, axis=1, keepdims=True)
                p = jnp.exp(s - m)
                den = jnp.sum(p, axis=1, keepdims=True)
                o = jnp.dot(p.astype(BF16), v_ref[0:kw, sl], preferred_element_type=F32) / den
                o_ref[:, sl] = o.astype(o_ref.dtype)


def _moba(q, kv, heads_per_step=4, blk=MOBA_BLOCK):
    b, s, c = q.shape
    n_heads = c // HEAD_DIM
    nblk = s // blk
    assert nblk % 2 == 0 and LANES % nblk == 0 and heads_per_step * nblk <= LANES
    hw = heads_per_step * HEAD_DIM
    ng = n_heads // heads_per_step
    et = (jnp.arange(s, dtype=jnp.int32)[:, None] // blk == jnp.arange(LANES, dtype=jnp.int32)[None, :]).astype(BF16)
    kern = functools.partial(_moba_kernel, heads=heads_per_step, seq=s, blk=blk)
    return pl.pallas_call(
        kern,
        grid=(b, ng, s // blk),
        in_specs=[pl.BlockSpec((None, blk, hw), lambda bi, g, t: (bi, t, g)),
                  pl.BlockSpec((None, s, hw), lambda bi, g, t: (bi, 0, g)),
                  pl.BlockSpec((None, s, hw), lambda bi, g, t: (bi, 0, g + ng)),
                  pl.BlockSpec((s, LANES), lambda bi, g, t: (0, 0))],
        out_specs=pl.BlockSpec((None, blk, hw), lambda bi, g, t: (bi, t, g)),
        out_shape=jax.ShapeDtypeStruct((b, s, c), BF16),
        compiler_params=_cparams(3),
        name="moba_attn",
    )(q, kv, kv, et)


def _router_kernel(h_ref, wr_ref, br_ref, idx_ref, w_ref, rank_ref, cnt_ref, *, tm):
    t = pl.program_id(0)

    @pl.when(t == 0)
    def _():
        cnt_ref[...] = jnp.zeros_like(cnt_ref)

    logits = lax.dot_general(wr_ref[...], h_ref[...], (((1,), (1,)), ((), ())),
                             preferred_element_type=F32) + br_ref[...]
    row = lax.broadcasted_iota(jnp.int32, (N_EXPERTS, tm), 0)
    rem = logits
    vals, idxs, hots = [], [], []
    for _ in range(TOP_K):
        mx = jnp.max(rem, axis=0, keepdims=True)
        ix = jnp.min(jnp.where(rem == mx, row, N_EXPERTS), axis=0, keepdims=True)
        hot = row == ix
        vals.append(mx)
        idxs.append(ix)
        hots.append(hot)
        rem = jnp.where(hot, -jnp.inf, rem)
    es = [jnp.exp(v - vals[0]) for v in vals]
    tot = es[0] + es[1] + es[2] + es[3]
    sel = jnp.zeros((N_EXPERTS, tm), F32)
    for hot in hots:
        sel = sel + hot.astype(F32)
    ri = lax.broadcasted_iota(jnp.int32, (tm, tm), 0)
    ci = lax.broadcasted_iota(jnp.int32, (tm, tm), 1)
    upper = jnp.where(ri <= ci, 1.0, 0.0).astype(BF16)
    incl = jnp.dot(sel.astype(BF16), upper, preferred_element_type=F32)
    base = cnt_ref[:, 0:1]
    rank_e = base + incl - sel
    ranks = [jnp.sum(jnp.where(hot, rank_e, 0.0), axis=0, keepdims=True) for hot in hots]
    idx_ref[...] = jnp.concatenate(idxs, axis=0)
    w_ref[...] = jnp.concatenate([e / tot for e in es], axis=0)
    rank_ref[...] = jnp.concatenate(ranks, axis=0).astype(jnp.int32)
    cnt_ref[...] = jnp.broadcast_to(base + incl[:, tm - 1:tm], cnt_ref.shape)


def _router(h_bf, w_router, b_router, tm=512):
    n_tok, d = h_bf.shape
    wr_t = w_router.T.astype(BF16)
    br = b_router.reshape(N_EXPERTS, 1).astype(F32)
    tok = lambda i: (0, i)
    const = lambda i: (0, 0)
    return pl.pallas_call(
        functools.partial(_router_kernel, tm=tm),
        grid=(n_tok // tm,),
        in_specs=[pl.BlockSpec((tm, d), lambda i: (i, 0)), pl.BlockSpec((N_EXPERTS, d), const),
                  pl.BlockSpec((N_EXPERTS, 1), const)],
        out_specs=[pl.BlockSpec((TOP_K, tm), tok), pl.BlockSpec((TOP_K, tm), tok), pl.BlockSpec((TOP_K, tm), tok),
                   pl.BlockSpec((N_EXPERTS, LANES), const)],
        out_shape=[jax.ShapeDtypeStruct((TOP_K, n_tok), jnp.int32), jax.ShapeDtypeStruct((TOP_K, n_tok), F32),
                   jax.ShapeDtypeStruct((TOP_K, n_tok), jnp.int32), jax.ShapeDtypeStruct((N_EXPERTS, LANES), F32)],
        compiler_params=_cparams(1),
        name="moe_router",
    )(h_bf, wr_t, br)


SUB = 256
SUPER = 1024
SUBS_PER_SUPER = SUPER // SUB


ROW_UNROLL = 8


def _moe_tables(counts, n_slots):
    n_super_max = N_EXPERTS + n_slots // SUPER
    n_sub_e = (counts + SUB - 1) // SUB
    xs_start = SUB * (jnp.cumsum(n_sub_e) - n_sub_e)
    n_sb_e = (n_sub_e + SUBS_PER_SUPER - 1) // SUBS_PER_SUPER
    per_e = jnp.maximum((n_sub_e + jnp.maximum(n_sb_e, 1) - 1) // jnp.maximum(n_sb_e, 1), 1)
    sb_end = jnp.cumsum(n_sb_e)
    sb_start = sb_end - n_sb_e
    total = sb_end[-1]
    g = jnp.arange(n_super_max, dtype=jnp.int32)
    gc = jnp.minimum(g, total - 1)
    e_of = jnp.minimum(jnp.searchsorted(sb_end, gc, side="right"), N_EXPERTS - 1).astype(jnp.int32)
    j_in = gc - sb_start[e_of]
    blk0 = xs_start[e_of] // SUB + per_e[e_of] * j_in
    nsub = jnp.clip(n_sub_e[e_of] - per_e[e_of] * j_in, 0, per_e[e_of])
    nsub_active = jnp.where(g < total, nsub, 0)
    ys_start = SUPER * sb_start
    xs_tail = xs_start + SUB * jnp.maximum(n_sub_e - 1, 0)
    i32 = lambda a: a.astype(jnp.int32)
    return dict(xs_start=i32(xs_start), ys_start=i32(ys_start), per=i32(per_e), xs_tail=i32(xs_tail),
                e_of=i32(e_of), blk0=i32(blk0), nsub=i32(nsub), nsub_active=i32(nsub_active), g_out=i32(gc))


def _slot_rows_kernel(xs_start_ref, ys_start_ref, per_ref, idx_ref, rank_ref, xr_ref, yr_ref):
    idx = idx_ref[...]
    rank = rank_ref[...]
    xs0 = jnp.zeros_like(idx)
    ys0 = jnp.zeros_like(idx)
    per = jnp.ones_like(idx)
    for e in range(N_EXPERTS):
        hit = idx == e
        xs0 = jnp.where(hit, xs_start_ref[e], xs0)
        ys0 = jnp.where(hit, ys_start_ref[e], ys0)
        per = jnp.where(hit, per_ref[e], per)
    sub = jnp.right_shift(rank, SUB.bit_length() - 1)
    q = jnp.floor((sub.astype(F32) + 0.5) / per.astype(F32)).astype(jnp.int32)
    xr_ref[...] = xs0 + rank
    yr_ref[...] = ys0 + q * SUPER + (rank - q * per * SUB)


def _slot_rows(top_idx, rank, tables, tm=2048):
    k, n_tok = top_idx.shape
    tm = min(tm, n_tok)
    blk = pl.BlockSpec((k, tm), lambda i, *_: (0, i))
    grid_spec = pltpu.PrefetchScalarGridSpec(
        num_scalar_prefetch=3, grid=(n_tok // tm,), in_specs=[blk, blk], out_specs=[blk, blk])
    return pl.pallas_call(
        _slot_rows_kernel,
        grid_spec=grid_spec,
        out_shape=[jax.ShapeDtypeStruct((k, n_tok), jnp.int32)] * 2,
        compiler_params=_cparams(1),
        name="moe_slot_rows",
    )(tables["xs_start"], tables["ys_start"], tables["per"], top_idx, rank)


def _dispatch_kernel(tail_ref, used_ref, row_ref, h_ref, xs_hbm, zbuf, sem, zsem, *, tm):
    def tail_copy(e):
        return pltpu.make_async_copy(zbuf, xs_hbm.at[pl.ds(pl.multiple_of(tail_ref[e], SUB), SUB)], zsem)

    @pl.when(pl.program_id(0) == 0)
    def _():
        zbuf[...] = jnp.zeros_like(zbuf)
        for e in range(N_EXPERTS):
            @pl.when(used_ref[e] > 0)
            def _():
                tail_copy(e).start()
        for e in range(N_EXPERTS):
            @pl.when(used_ref[e] > 0)
            def _():
                tail_copy(e).wait()

    def row_copy(t, k):
        return pltpu.make_async_copy(h_ref.at[pl.ds(t, 1)], xs_hbm.at[pl.ds(row_ref[k, t], 1)], sem)

    def issue(i, carry):
        t0 = pl.multiple_of(i * ROW_UNROLL, ROW_UNROLL)
        for u in range(ROW_UNROLL):
            for k in range(TOP_K):
                row_copy(t0 + u, k).start()
        return carry

    def drain(i, carry):
        t0 = pl.multiple_of(i * ROW_UNROLL, ROW_UNROLL)
        for u in range(ROW_UNROLL):
            for k in range(TOP_K):
                row_copy(t0 + u, k).wait()
        return carry

    lax.fori_loop(0, tm // ROW_UNROLL, issue, 0)
    lax.fori_loop(0, tm // ROW_UNROLL, drain, 0)


def _dispatch(hp, xs_row, tables, counts, n_rows, tm=256):
    n_tok, dp = hp.shape
    grid_spec = pltpu.PrefetchScalarGridSpec(
        num_scalar_prefetch=2,
        grid=(n_tok // tm,),
        in_specs=[pl.BlockSpec((TOP_K, tm), lambda i, *_: (0, i), memory_space=pltpu.SMEM),
                  pl.BlockSpec((tm, dp), lambda i, *_: (i, 0))],
        out_specs=pl.BlockSpec(memory_space=pl.ANY),
        scratch_shapes=[pltpu.VMEM((SUB, dp), hp.dtype), pltpu.SemaphoreType.DMA, pltpu.SemaphoreType.DMA],
    )
    return pl.pallas_call(
        functools.partial(_dispatch_kernel, tm=tm),
        grid_spec=grid_spec,
        out_shape=jax.ShapeDtypeStruct((n_rows, dp), hp.dtype),
        compiler_params=_cparams(1),
        name="moe_dispatch",
    )(tables["xs_tail"], counts, xs_row, hp)


def _ffn_kernel(e_ref, blk0_ref, nsub_ref, act_ref, out_ref_idx, x0, x1, x2, x3, wg_ref, wu_ref, bg_ref, bu_ref,
                wd_ref, bd_ref, y_ref, xb, *, tf):
    del e_ref, blk0_ref, nsub_ref, out_ref_idx
    g = pl.program_id(0)
    j = pl.program_id(1)
    nact = act_ref[g]

    @pl.when(j == 0)
    def _():
        half = xb.shape[1] // 2
        for s, x_ref in enumerate((x0, x1, x2, x3)):
            @pl.when(s < nact)
            def _():
                lo, hi = _unpack_pairs(x_ref[...])
                xb[s * SUB:(s + 1) * SUB, :half] = lo
                xb[s * SUB:(s + 1) * SUB, half:] = hi

    for n in range(1, SUBS_PER_SUPER + 1):
        @pl.when(nact == n)
        def _():
            m = n * SUB
            x = xb[0:m, :]
            gate = jnp.dot(x, wg_ref[...].astype(BF16), preferred_element_type=F32) + bg_ref[...]
            lin = jnp.dot(x, wu_ref[...].astype(BF16), preferred_element_type=F32) + bu_ref[...]
            gate = jnp.minimum(gate, SWIGLU_LIMIT)
            lin = jnp.clip(lin, -SWIGLU_LIMIT, SWIGLU_LIMIT)
            hid = (lin + 1.0) * gate * jax.nn.sigmoid(SWIGLU_ALPHA * gate)

            @pl.when(j == 0)
            def _():
                y_ref[0:m, :] = jnp.broadcast_to(bd_ref[...], (m, y_ref.shape[1]))

            y_ref[0:m, :] += jnp.dot(hid.astype(BF16), wd_ref[...].astype(BF16), preferred_element_type=F32)


def _expert_ffn(xs, tables, layer, w_gu, b_gu, w_down, b_down, n_super_max, tf=256):
    e_of, blk0, nsub, nsub_active, g_out = (tables[k] for k in ("e_of", "blk0", "nsub", "nsub_active", "g_out"))
    n_rows, dp = xs.shape
    d = 2 * dp
    n_we = w_gu.shape[0] * N_EXPERTS
    e_of = e_of + layer * N_EXPERTS
    w_gu = w_gu.reshape(n_we, d, 2 * D_FF)
    w_down = w_down.reshape(n_we, D_FF, d)
    b_gu = b_gu.reshape(n_we, 1, 2 * D_FF)
    b_down = b_down.reshape(n_we, 1, d)
    n_ff = D_FF // tf
    last = n_ff - 1

    def x_spec(s):
        return pl.BlockSpec(
            (SUB, dp), lambda g, j, e, b0, ns, na, go: (b0[g] + jnp.minimum(s, jnp.maximum(ns[g], 1) - 1), 0))

    def jj(j, na, g):
        return jnp.where(na[g] > 0, j, last)

    in_specs = [x_spec(s) for s in range(SUBS_PER_SUPER)] + [
        pl.BlockSpec((None, d, tf), lambda g, j, e, b0, ns, na, go: (e[g], 0, jj(j, na, g))),
        pl.BlockSpec((None, d, tf), lambda g, j, e, b0, ns, na, go: (e[g], 0, n_ff + jj(j, na, g))),
        pl.BlockSpec((None, 1, tf), lambda g, j, e, b0, ns, na, go: (e[g], 0, jj(j, na, g))),
        pl.BlockSpec((None, 1, tf), lambda g, j, e, b0, ns, na, go: (e[g], 0, n_ff + jj(j, na, g))),
        pl.BlockSpec((None, tf, d), lambda g, j, e, b0, ns, na, go: (e[g], jj(j, na, g), 0)),
        pl.BlockSpec((None, 1, d), lambda g, j, e, b0, ns, na, go: (e[g], 0, 0)),
    ]
    grid_spec = pltpu.PrefetchScalarGridSpec(
        num_scalar_prefetch=5,
        grid=(n_super_max, n_ff),
        in_specs=in_specs,
        out_specs=pl.BlockSpec((SUPER, d), lambda g, j, e, b0, ns, na, go: (go[g], 0)),
        scratch_shapes=[pltpu.VMEM((SUPER, d), BF16)],
    )
    return pl.pallas_call(
        functools.partial(_ffn_kernel, tf=tf),
        grid_spec=grid_spec,
        out_shape=jax.ShapeDtypeStruct((n_super_max * SUPER, d), F32),
        compiler_params=_cparams(2),
        name="moe_ffn",
    )(e_of, blk0, nsub, nsub_active, g_out, xs, xs, xs, xs, w_gu, w_gu, b_gu, b_gu, w_down, b_down)


def _combine_kernel(row_ref, w_ref, h_ref, g_ref, b_ref, ys_hbm, of_ref, ob_ref, op_ref, ybuf, sem, *, tm):
    def row_copy(t, k):
        return pltpu.make_async_copy(ys_hbm.at[pl.ds(row_ref[k, t], 1)], ybuf.at[k, pl.ds(t, 1)], sem)

    def issue(i, carry):
        t0 = pl.multiple_of(i * ROW_UNROLL, ROW_UNROLL)
        for u in range(ROW_UNROLL):
            for k in range(TOP_K):
                row_copy(t0 + u, k).start()
        return carry

    def drain(i, carry):
        t0 = pl.multiple_of(i * ROW_UNROLL, ROW_UNROLL)
        for u in range(ROW_UNROLL):
            for k in range(TOP_K):
                row_copy(t0 + u, k).wait()
        return carry

    lax.fori_loop(0, tm // ROW_UNROLL, issue, 0)
    lax.fori_loop(0, tm // ROW_UNROLL, drain, 0)
    w = w_ref[...]
    ffn = w[:, 0:1] * ybuf[0]
    for k in range(1, TOP_K):
        ffn = ffn + w[:, k:k + 1] * ybuf[k]
    y = _layer_norm_rows(DEEPNORM_ALPHA * h_ref[...] + ffn, g_ref[...], b_ref[...])
    of_ref[...] = y
    ob_ref[...] = y.astype(ob_ref.dtype)
    op_ref[...] = _pack_pairs(y)


def _combine_ln(ys, ys_row, w_t, h, g, b, tm=128):
    n_tok, d = h.shape
    row = lambda i: (i, 0)
    const = lambda i: (0, 0)
    return pl.pallas_call(
        functools.partial(_combine_kernel, tm=tm),
        grid=(n_tok // tm,),
        in_specs=[pl.BlockSpec((TOP_K, tm), lambda i: (0, i), memory_space=pltpu.SMEM),
                  pl.BlockSpec((tm, TOP_K), row), pl.BlockSpec((tm, d), row),
                  pl.BlockSpec((1, d), const), pl.BlockSpec((1, d), const), pl.BlockSpec(memory_space=pl.ANY)],
        out_specs=[pl.BlockSpec((tm, d), row), pl.BlockSpec((tm, d), row), pl.BlockSpec((tm, d // 2), row)],
        out_shape=[jax.ShapeDtypeStruct((n_tok, d), F32), jax.ShapeDtypeStruct((n_tok, d), BF16),
                   jax.ShapeDtypeStruct((n_tok, d // 2), jnp.uint32)],
        scratch_shapes=[pltpu.VMEM((TOP_K, tm, d), F32), pltpu.SemaphoreType.DMA],
        compiler_params=_cparams(1),
        name="moe_combine_ln",
    )(ys_row, w_t, h, g.reshape(1, d), b.reshape(1, d), ys)


def _moe_ln(h, h_bf, h_pk, layer, w_router, b_router, w_gu, b_gu, w_down, b_down, ln_g, ln_b):
    n_tok, d = h.shape
    n_slots = n_tok * TOP_K
    top_idx, top_w, rank, cnt = _router(h_bf, w_router, b_router)
    counts = cnt[:, 0].astype(jnp.int32)
    tables = _moe_tables(counts, n_slots)
    xs_row, ys_row = _slot_rows(top_idx, rank, tables)
    n_super_max = N_EXPERTS + n_slots // SUPER
    n_rows = n_slots + N_EXPERTS * SUB
    xs = _dispatch(h_pk, xs_row, tables, counts, n_rows)
    ys = _expert_ffn(xs, tables, layer, w_gu, b_gu, w_down, b_down, n_super_max)
    return _combine_ln(ys, ys_row, top_w.T, h, ln_g, ln_b)


def _dilated_mix(x3, w_qkv, tabs):
    b, s, d = x3.shape
    n_tok = b * s
    width = A_HEADS * HEAD_DIM
    outs, lses = [], []
    for g, (window, dil) in enumerate(A_PATTERNS):
        assert window // dil == 128
        L = s // dil
        xp = x3.reshape(b, L, dil, d).transpose(0, 2, 1, 3).reshape(n_tok, d).astype(BF16)
        tabs_p = tabs.reshape(3, L, dil, HEAD_DIM).transpose(0, 2, 1, 3).reshape(3, s, HEAD_DIM)
        qkv = _proj(xp, w_qkv, tabs_p, col0=g * 3 * width, n_out=3 * width, pos_period=s, rope_mod=3, rope_cnt=2)
        o, lse = _band_attn(qkv.reshape(b * dil, L, 3 * width))
        outs.append(o.reshape(b, dil, L, width).transpose(0, 2, 1, 3).reshape(n_tok, width))
        lses.append(lse.reshape(b, dil, L, A_HEADS).transpose(0, 2, 1, 3).reshape(n_tok, A_HEADS))
    return _mix(outs, lses)


def kernel(x, a_w_qkv, a_w_o, kv_w, b_w_q, b_w_o, router_w, router_b, moe_w_gate_up, moe_b_gate_up, moe_w_down,
           moe_b_down, ln1_g, ln1_b, ln2_g, ln2_b):
    b, s, d = x.shape
    n_tok = b * s
    tabs = _rope_tables(s)
    h = x.reshape(n_tok, d)
    h_bf = h.astype(BF16)
    kv = None
    for layer in range(DEPTH):
        if layer < N_A_LAYERS:
            mix = _dilated_mix(h.reshape(b, s, d), a_w_qkv[layer], tabs)
            w_o = a_w_o[layer]
        else:
            j = layer - N_A_LAYERS
            if layer == N_A_LAYERS:
                kv = _proj(h_bf, kv_w, tabs, col0=0, n_out=2 * B_HEADS * HEAD_DIM, pos_period=s, rope_mod=2,
                           rope_cnt=1)
            q = _proj(h_bf, b_w_q[j], tabs, col0=0, n_out=B_HEADS * HEAD_DIM, pos_period=s, rope_mod=1, rope_cnt=1)
            mix = _moba(q.reshape(b, s, -1), kv.reshape(b, s, -1)).reshape(n_tok, -1)
            w_o = b_w_o[j]
        h, h_bf, h_pk = _oproj_ln(mix, w_o.astype(BF16), h, ln1_g[layer], ln1_b[layer])
        h, h_bf, _ = _moe_ln(h, h_bf, h_pk, layer, router_w[layer], router_b[layer], moe_w_gate_up, moe_b_gate_up,
                             moe_w_down, moe_b_down, ln2_g[layer], ln2_b[layer])
    return h.reshape(b, s, d)
```

```python
import functools

import jax
import jax.numpy as jnp
from jax import lax
from jax.experimental import pallas as pl
from jax.experimental.pallas import tpu as pltpu

D_MODEL = 2048
SEQ = 2048
DEPTH = 2
HEAD_DIM = 128
ROT_DIM = HEAD_DIM // 4
ROPE_THETA = 500000.0
A_PATTERNS = ((128, 1), (512, 4), (2048, 16))
A_GROUPS = len(A_PATTERNS)
A_HEADS = 16
B_HEADS = 16
MOBA_BLOCK = 256
MOBA_TOPK = 3
N_EXPERTS = 32
TOP_K = 4
D_FF = 2048
SWIGLU_LIMIT = 7.0
SWIGLU_ALPHA = 1.702
N_A_LAYERS = DEPTH // 2
DEEPNORM_ALPHA = (2 * DEPTH) ** 0.25
LN_EPS = 1e-5
NEG_INF = -1e30

LANES = 128
V7X_VMEM_BYTES = 64 * 1024 * 1024
VMEM_LIMIT = 56 * 1024 * 1024
FFN_VMEM_LIMIT = 61 * 1024 * 1024

BF16 = jnp.bfloat16
F32 = jnp.float32


def _cparams(n_axes, vmem_limit=VMEM_LIMIT):
    return pltpu.CompilerParams(dimension_semantics=("arbitrary",) * n_axes, vmem_limit_bytes=vmem_limit)


def _rope_tables(seq):
    half = ROT_DIM // 2
    inv = ROPE_THETA ** (-jnp.arange(0, ROT_DIM, 2, dtype=F32) / ROT_DIM)
    ang = jnp.arange(seq, dtype=F32)[:, None] * inv[None, :]
    cos, sin = jnp.cos(ang), jnp.sin(ang)
    zeros = jnp.zeros((seq, HEAD_DIM - ROT_DIM), F32)
    c = jnp.concatenate([cos, cos, jnp.ones((seq, HEAD_DIM - ROT_DIM), F32)], axis=1)
    s1 = jnp.concatenate([-sin, jnp.zeros((seq, half), F32), zeros], axis=1)
    s2 = jnp.concatenate([jnp.zeros((seq, half), F32), sin, zeros], axis=1)
    return jnp.stack([c, s1, s2])


def _rope_tile(x, c, s1, s2):
    half = ROT_DIM // 2
    return x * c + pltpu.roll(x, HEAD_DIM - half, 1) * s1 + pltpu.roll(x, half, 1) * s2


def _proj_kernel(a_ref, w_ref, tab_ref, o_ref, acc_s, *, tn):
    @pl.when(pl.program_id(0) == 0)
    def _():
        acc_s[...] = jnp.zeros_like(acc_s)

    prev = acc_s[...]
    c, s1, s2 = tab_ref[0], tab_ref[1], tab_ref[2]
    for h in range(tn // HEAD_DIM):
        sl = slice(h * HEAD_DIM, (h + 1) * HEAD_DIM)
        o_ref[:, sl] = _rope_tile(prev[:, sl], c, s1, s2).astype(o_ref.dtype)
    acc_s[...] = jnp.dot(a_ref[...], w_ref[...].astype(BF16), preferred_element_type=F32)


def _proj(a, w, tabs, *, col0, n_out, pos_period, rope_mod, rope_cnt, tm=1024, tn=512):
    m, k = a.shape
    assert m % tm == 0 and n_out % tn == 0 and col0 % tn == 0 and pos_period % tm == 0 and D_MODEL % tn == 0
    jb = col0 // tn
    per = pos_period // tm
    ident = jnp.stack([jnp.ones_like(tabs[0]), jnp.zeros_like(tabs[0]), jnp.zeros_like(tabs[0])])
    tabs2 = jnp.stack([tabs, ident])

    nj = n_out // tn
    n_tiles = (m // tm) * nj

    def plain(j):
        return jnp.where(((j * tn) // D_MODEL) % rope_mod < rope_cnt, 0, 1)

    cur = lambda t: jnp.minimum(t, n_tiles - 1)
    fin = lambda t: jnp.maximum(t - 1, 0)
    return pl.pallas_call(
        functools.partial(_proj_kernel, tn=tn),
        grid=(n_tiles + 1,),
        in_specs=[
            pl.BlockSpec((tm, k), lambda t: (cur(t) // nj, 0)),
            pl.BlockSpec((k, tn), lambda t: (0, cur(t) % nj + jb)),
            pl.BlockSpec((None, 3, tm, HEAD_DIM), lambda t: (plain(fin(t) % nj), 0, (fin(t) // nj) % per, 0)),
        ],
        out_specs=pl.BlockSpec((tm, tn), lambda t: (fin(t) // nj, fin(t) % nj)),
        out_shape=jax.ShapeDtypeStruct((m, n_out), BF16),
        scratch_shapes=[pltpu.VMEM((tm, tn), F32)],
        compiler_params=_cparams(1),
        name="proj_rope",
    )(a, w, tabs2)


def _band_attn_kernel(*refs, has_prev, n_heads, blk):
    if has_prev:
        q_ref, kp_ref, kc_ref, vp_ref, vc_ref, o_ref, lse_ref = refs
    else:
        q_ref, kc_ref, vc_ref, o_ref, lse_ref = refs
    i = pl.program_id(1)
    nk = 2 * blk if has_prev else blk
    qi = lax.broadcasted_iota(jnp.int32, (blk, nk), 0)
    ki = lax.broadcasted_iota(jnp.int32, (blk, nk), 1) - (nk - blk)
    diff = qi - ki
    mask = (diff >= 0) & (diff <= blk)
    if has_prev:
        mask = mask & ((i > 0) | (ki >= 0))
    scale = HEAD_DIM ** -0.5
    lane = lax.broadcasted_iota(jnp.int32, (blk, n_heads), 1)
    lse_all = jnp.zeros((blk, n_heads), F32)
    for h in range(n_heads):
        sl = slice(h * HEAD_DIM, (h + 1) * HEAD_DIM)
        q = q_ref[:, sl]
        if has_prev:
            k = jnp.concatenate([kp_ref[:, sl], kc_ref[:, sl]], axis=0)
            v = jnp.concatenate([vp_ref[:, sl], vc_ref[:, sl]], axis=0)
        else:
            k, v = kc_ref[:, sl], vc_ref[:, sl]
        s = lax.dot_general(q, k, (((1,), (1,)), ((), ())), preferred_element_type=F32) * scale
        s = jnp.where(mask, s, NEG_INF)
        m = jnp.max(s, axis=1, keepdims=True)
        p = jnp.exp(s - m)
        den = jnp.sum(p, axis=1, keepdims=True)
        o = jnp.dot(p.astype(BF16), v, preferred_element_type=F32) / den
        o_ref[:, sl] = o.astype(o_ref.dtype)
        lse_all = jnp.where(lane == h, m + jnp.log(den), lse_all)
    lse_ref[...] = lse_all


def _band_attn(qkv, n_heads=A_HEADS, blk=128):
    n, L, c3 = qkv.shape
    c = c3 // 3
    assert c == n_heads * HEAD_DIM and L % blk == 0
    nb = L // blk
    has_prev = True
    spec = lambda col, prev: pl.BlockSpec(
        (None, blk, c), (lambda s, i: (s, jnp.maximum(i - 1, 0), col)) if prev else (lambda s, i: (s, i, col)))
    if has_prev:
        in_specs = [spec(0, False), spec(1, True), spec(1, False), spec(2, True), spec(2, False)]
        args = (qkv,) * 5
    else:
        in_specs = [spec(0, False), spec(1, False), spec(2, False)]
        args = (qkv,) * 3
    kern = functools.partial(_band_attn_kernel, has_prev=has_prev, n_heads=n_heads, blk=blk)
    return pl.pallas_call(
        kern,
        grid=(n, nb),
        in_specs=in_specs,
        out_specs=[pl.BlockSpec((None, blk, c), lambda s, i: (s, i, 0)),
                   pl.BlockSpec((None, blk, n_heads), lambda s, i: (s, i, 0))],
        out_shape=[jax.ShapeDtypeStruct((n, L, c), BF16), jax.ShapeDtypeStruct((n, L, n_heads), F32)],
        compiler_params=_cparams(2),
        name="band_attn",
    )(*args)


def _mix_kernel(o0, o1, o2, l0, l1, l2, out_ref, *, n_heads):
    ls = [l0[...], l1[...], l2[...]]
    mx = jnp.maximum(jnp.maximum(ls[0], ls[1]), ls[2])
    es = [jnp.exp(l - mx) for l in ls]
    tot = es[0] + es[1] + es[2]
    ws = [e / tot for e in es]
    os_ = [o0, o1, o2]
    for h in range(n_heads):
        sl = slice(h * HEAD_DIM, (h + 1) * HEAD_DIM)
        acc = ws[0][:, h:h + 1] * os_[0][:, sl].astype(F32)
        acc += ws[1][:, h:h + 1] * os_[1][:, sl].astype(F32)
        acc += ws[2][:, h:h + 1] * os_[2][:, sl].astype(F32)
        out_ref[:, sl] = acc.astype(out_ref.dtype)


def _mix(os_, ls, tm=256, n_heads=A_HEADS):
    m, c = os_[0].shape
    ospec = pl.BlockSpec((tm, c), lambda i: (i, 0))
    lspec = pl.BlockSpec((tm, n_heads), lambda i: (i, 0))
    return pl.pallas_call(
        functools.partial(_mix_kernel, n_heads=n_heads),
        grid=(m // tm,),
        in_specs=[ospec] * 3 + [lspec] * 3,
        out_specs=ospec,
        out_shape=jax.ShapeDtypeStruct((m, c), BF16),
        compiler_params=_cparams(1),
        name="group_mix",
    )(*os_, *ls)


def _layer_norm_rows(y, g, b):
    mu = jnp.mean(y, axis=1, keepdims=True)
    yc = y - mu
    var = jnp.mean(yc * yc, axis=1, keepdims=True)
    return yc * lax.rsqrt(var + LN_EPS) * g + b


def _pack_pairs(y):
    half = y.shape[1] // 2
    lo = lax.bitcast_convert_type(y[:, :half].astype(BF16).astype(F32), jnp.uint32)
    hi = lax.bitcast_convert_type(y[:, half:].astype(BF16).astype(F32), jnp.uint32)
    return hi | (lo >> 16)


def _unpack_pairs(u):
    lo = lax.bitcast_convert_type(u << 16, F32).astype(BF16)
    hi = lax.bitcast_convert_type(u & jnp.uint32(0xFFFF0000), F32).astype(BF16)
    return lo, hi


def _oproj_ln_kernel(a_ref, w_ref, h_ref, g_ref, b_ref, of_ref, ob_ref, op_ref):
    acc = jnp.dot(a_ref[...], w_ref[...], preferred_element_type=F32)
    y = _layer_norm_rows(DEEPNORM_ALPHA * h_ref[...] + acc, g_ref[...], b_ref[...])
    of_ref[...] = y
    ob_ref[...] = y.astype(ob_ref.dtype)
    op_ref[...] = _pack_pairs(y)


def _oproj_ln(a, w_bf, h, g, b, tm=512):
    m, k = a.shape
    d = w_bf.shape[1]
    row = lambda i: (i, 0)
    const = lambda i: (0, 0)
    return pl.pallas_call(
        _oproj_ln_kernel,
        grid=(m // tm,),
        in_specs=[pl.BlockSpec((tm, k), row), pl.BlockSpec((k, d), const), pl.BlockSpec((tm, d), row),
                  pl.BlockSpec((1, d), const), pl.BlockSpec((1, d), const)],
        out_specs=[pl.BlockSpec((tm, d), row), pl.BlockSpec((tm, d), row), pl.BlockSpec((tm, d // 2), row)],
        out_shape=[jax.ShapeDtypeStruct((m, d), F32), jax.ShapeDtypeStruct((m, d), BF16),
                   jax.ShapeDtypeStruct((m, d // 2), jnp.uint32)],
        compiler_params=_cparams(1),
        name="oproj_ln",
    )(a, w_bf, h, g.reshape(1, d), b.reshape(1, d))


def _moba_kernel(q_ref, k_ref, v_ref, et_ref, o_ref, km_s, *, heads, seq, blk):
    c = pl.program_id(2)
    nblk = seq // blk
    hw = heads * HEAD_DIM
    scale = HEAD_DIM ** -0.5

    @pl.when(c == 0)
    def _():
        ar = lax.broadcasted_iota(jnp.int32, (16, seq), 0)
        ac = lax.broadcasted_iota(jnp.int32, (16, seq), 1)
        avg = jnp.where(ac // blk == ar, 1.0 / blk, 0.0).astype(BF16)
        kmean = jnp.dot(avg, k_ref[...], preferred_element_type=F32)[0:nblk]
        kmt = jnp.concatenate([kmean] * (LANES // nblk), axis=0)
        kr = lax.broadcasted_iota(jnp.int32, (LANES, hw), 0)
        kc = lax.broadcasted_iota(jnp.int32, (LANES, hw), 1)
        km_s[...] = jnp.where(kr // nblk == kc // HEAD_DIM, kmt, 0.0).astype(BF16)

    gate = lax.dot_general(q_ref[...], km_s[...], (((1,), (1,)), ((), ())),
                           preferred_element_type=F32)
    lane = lax.broadcasted_iota(jnp.int32, (blk, LANES), 1)
    n = lane % nblk
    valid = (n < c) & (lane < heads * nblk)
    g = jnp.where(valid, gate, NEG_INF)
    cnt = jnp.zeros((blk, LANES), jnp.int32)
    for sh in range(1, nblk):
        lo = pltpu.roll(g, sh, 1)
        cnt = cnt + jnp.where((n >= sh) & (lo >= g), 1, 0)
        hi = pltpu.roll(g, LANES - sh, 1)
        cnt = cnt + jnp.where((n + sh < nblk) & (hi > g), 1, 0)
    sel = (cnt < MOBA_TOPK) & valid
    bias = jnp.where(sel | (n == c), 0.0, NEG_INF).astype(F32)

    for nb in range(2, nblk + 1, 2):
        @pl.when(2 * (c // 2 + 1) == nb)
        def _():
            kw = nb * blk
            past = kw - 2 * blk
            qpos = c * blk + lax.broadcasted_iota(jnp.int32, (blk, 2 * blk), 0)
            kpos = past + lax.broadcasted_iota(jnp.int32, (blk, 2 * blk), 1)
            causal = kpos <= qpos
            et = et_ref[0:kw, :]
            for h in range(heads):
                sl = slice(h * HEAD_DIM, (h + 1) * HEAD_DIM)
                bias_h = bias if h == 0 else pltpu.roll(bias, LANES - h * nblk, 1)
                q_aug = jnp.concatenate([q_ref[:, sl], bias_h.astype(BF16)], axis=1)
                k_aug = jnp.concatenate([k_ref[0:kw, sl], et], axis=1)
                s = lax.dot_general(q_aug, k_aug, (((1,), (1,)), ((), ())), preferred_element_type=F32) * scale
                tail = jnp.where(causal, s[:, past:], NEG_INF)
                s = tail if past == 0 else jnp.concatenate([s[:, :past], tail], axis=1)
                m = jnp.max(s, axis=1, keepdims=True)
                p = jnp.exp(s - m)
                den = jnp.sum(p, axis=1, keepdims=True)
                o = jnp.dot(p.astype(BF16), v_ref[0:kw, sl], preferred_element_type=F32) / den
                o_ref[:, sl] = o.astype(o_ref.dtype)


def _moba(q, kv, heads_per_step=4, blk=MOBA_BLOCK):
    b, s, c = q.shape
    n_heads = c // HEAD_DIM
    nblk = s // blk
    assert nblk % 2 == 0 and LANES % nblk == 0 and heads_per_step * nblk <= LANES
    hw = heads_per_step * HEAD_DIM
    ng = n_heads // heads_per_step
    et = (jnp.arange(s, dtype=jnp.int32)[:, None] // blk == jnp.arange(LANES, dtype=jnp.int32)[None, :]).astype(BF16)
    kern = functools.partial(_moba_kernel, heads=heads_per_step, seq=s, blk=blk)
    return pl.pallas_call(
        kern,
        grid=(b, ng, s // blk),
        in_specs=[pl.BlockSpec((None, blk, hw), lambda bi, g, t: (bi, t, g)),
                  pl.BlockSpec((None, s, hw), lambda bi, g, t: (bi, 0, g)),
                  pl.BlockSpec((None, s, hw), lambda bi, g, t: (bi, 0, g + ng)),
                  pl.BlockSpec((s, LANES), lambda bi, g, t: (0, 0))],
        out_specs=pl.BlockSpec((None, blk, hw), lambda bi, g, t: (bi, t, g)),
        out_shape=jax.ShapeDtypeStruct((b, s, c), BF16),
        scratch_shapes=[pltpu.VMEM((LANES, hw), BF16)],
        compiler_params=_cparams(3),
        name="moba_attn",
    )(q, kv, kv, et)


def _router_kernel(h_ref, wr_ref, br_ref, idx_ref, w_ref, rank_ref, cnt_ref, *, tm):
    t = pl.program_id(0)

    @pl.when(t == 0)
    def _():
        cnt_ref[...] = jnp.zeros_like(cnt_ref)

    logits = lax.dot_general(wr_ref[...], h_ref[...], (((1,), (1,)), ((), ())),
                             preferred_element_type=F32) + br_ref[...]
    row = lax.broadcasted_iota(jnp.int32, (N_EXPERTS, tm), 0)
    rem = logits
    vals, idxs, hots = [], [], []
    for _ in range(TOP_K):
        mx = jnp.max(rem, axis=0, keepdims=True)
        ix = jnp.min(jnp.where(rem == mx, row, N_EXPERTS), axis=0, keepdims=True)
        hot = row == ix
        vals.append(mx)
        idxs.append(ix)
        hots.append(hot)
        rem = jnp.where(hot, -jnp.inf, rem)
    es = [jnp.exp(v - vals[0]) for v in vals]
    tot = es[0] + es[1] + es[2] + es[3]
    sel = jnp.zeros((N_EXPERTS, tm), F32)
    for hot in hots:
        sel = sel + hot.astype(F32)
    ri = lax.broadcasted_iota(jnp.int32, (tm, tm), 0)
    ci = lax.broadcasted_iota(jnp.int32, (tm, tm), 1)
    upper = jnp.where(ri <= ci, 1.0, 0.0).astype(BF16)
    incl = jnp.dot(sel.astype(BF16), upper, preferred_element_type=F32)
    base = cnt_ref[:, 0:1]
    rank_e = base + incl - sel
    ranks = [jnp.sum(jnp.where(hot, rank_e, 0.0), axis=0, keepdims=True) for hot in hots]
    idx_ref[...] = jnp.concatenate(idxs, axis=0)
    w_ref[...] = jnp.concatenate([e / tot for e in es], axis=0)
    rank_ref[...] = jnp.concatenate(ranks, axis=0).astype(jnp.int32)
    cnt_ref[...] = jnp.broadcast_to(base + incl[:, tm - 1:tm], cnt_ref.shape)


def _router(h_bf, w_router, b_router, tm=512):
    n_tok, d = h_bf.shape
    wr_t = w_router.T.astype(BF16)
    br = b_router.reshape(N_EXPERTS, 1).astype(F32)
    tok = lambda i: (0, i)
    const = lambda i: (0, 0)
    return pl.pallas_call(
        functools.partial(_router_kernel, tm=tm),
        grid=(n_tok // tm,),
        in_specs=[pl.BlockSpec((tm, d), lambda i: (i, 0)), pl.BlockSpec((N_EXPERTS, d), const),
                  pl.BlockSpec((N_EXPERTS, 1), const)],
        out_specs=[pl.BlockSpec((TOP_K, tm), tok), pl.BlockSpec((TOP_K, tm), tok), pl.BlockSpec((TOP_K, tm), tok),
                   pl.BlockSpec((N_EXPERTS, LANES), const)],
        out_shape=[jax.ShapeDtypeStruct((TOP_K, n_tok), jnp.int32), jax.ShapeDtypeStruct((TOP_K, n_tok), F32),
                   jax.ShapeDtypeStruct((TOP_K, n_tok), jnp.int32), jax.ShapeDtypeStruct((N_EXPERTS, LANES), F32)],
        compiler_params=_cparams(1),
        name="moe_router",
    )(h_bf, wr_t, br)


SUB = 256
SUPER = 1024
SUBS_PER_SUPER = SUPER // SUB


ROW_UNROLL = 8
ROW_GRAN = 128


def _moe_tables(counts, n_slots):
    n_super_max = N_EXPERTS + n_slots // SUPER
    n_sub_e = (counts + SUB - 1) // SUB
    xs_start = SUB * (jnp.cumsum(n_sub_e) - n_sub_e)
    n_sb_e = (n_sub_e + SUBS_PER_SUPER - 1) // SUBS_PER_SUPER
    per_e = jnp.maximum((n_sub_e + jnp.maximum(n_sb_e, 1) - 1) // jnp.maximum(n_sb_e, 1), 1)
    sb_end = jnp.cumsum(n_sb_e)
    sb_start = sb_end - n_sb_e
    total = sb_end[-1]
    g = jnp.arange(n_super_max, dtype=jnp.int32)
    gc = jnp.minimum(g, total - 1)
    e_of = jnp.minimum(jnp.searchsorted(sb_end, gc, side="right"), N_EXPERTS - 1).astype(jnp.int32)
    j_in = gc - sb_start[e_of]
    blk0 = xs_start[e_of] // SUB + per_e[e_of] * j_in
    nsub = jnp.clip(n_sub_e[e_of] - per_e[e_of] * j_in, 0, per_e[e_of])
    rows = jnp.clip(counts[e_of] - SUB * per_e[e_of] * j_in, 0, SUB * nsub)
    nsub_active = jnp.where(g < total, (rows + ROW_GRAN - 1) // ROW_GRAN, 0)
    ys_start = SUPER * sb_start
    xs_tail = xs_start + SUB * jnp.maximum(n_sub_e - 1, 0)
    i32 = lambda a: a.astype(jnp.int32)
    return dict(xs_start=i32(xs_start), ys_start=i32(ys_start), per=i32(per_e), xs_tail=i32(xs_tail),
                e_of=i32(e_of), blk0=i32(blk0), nsub=i32(nsub), nsub_active=i32(nsub_active), g_out=i32(gc))


def _slot_rows_kernel(xs_start_ref, ys_start_ref, per_ref, idx_ref, rank_ref, xr_ref, yr_ref):
    idx = idx_ref[...]
    rank = rank_ref[...]
    xs0 = jnp.zeros_like(idx)
    ys0 = jnp.zeros_like(idx)
    per = jnp.ones_like(idx)
    for e in range(N_EXPERTS):
        hit = idx == e
        xs0 = jnp.where(hit, xs_start_ref[e], xs0)
        ys0 = jnp.where(hit, ys_start_ref[e], ys0)
        per = jnp.where(hit, per_ref[e], per)
    sub = jnp.right_shift(rank, SUB.bit_length() - 1)
    q = jnp.floor((sub.astype(F32) + 0.5) / per.astype(F32)).astype(jnp.int32)
    xr_ref[...] = xs0 + rank
    yr_ref[...] = ys0 + q * SUPER + (rank - q * per * SUB)


def _slot_rows(top_idx, rank, tables, tm=2048):
    k, n_tok = top_idx.shape
    tm = min(tm, n_tok)
    blk = pl.BlockSpec((k, tm), lambda i, *_: (0, i))
    grid_spec = pltpu.PrefetchScalarGridSpec(
        num_scalar_prefetch=3, grid=(n_tok // tm,), in_specs=[blk, blk], out_specs=[blk, blk])
    return pl.pallas_call(
        _slot_rows_kernel,
        grid_spec=grid_spec,
        out_shape=[jax.ShapeDtypeStruct((k, n_tok), jnp.int32)] * 2,
        compiler_params=_cparams(1),
        name="moe_slot_rows",
    )(tables["xs_start"], tables["ys_start"], tables["per"], top_idx, rank)


def _dispatch_kernel(tail_ref, used_ref, row_ref, h_ref, xs_hbm, zbuf, sem, zsem, *, tm):
    def tail_copy(e):
        return pltpu.make_async_copy(zbuf, xs_hbm.at[pl.ds(pl.multiple_of(tail_ref[e], SUB), SUB)], zsem)

    @pl.when(pl.program_id(0) == 0)
    def _():
        zbuf[...] = jnp.zeros_like(zbuf)
        for e in range(N_EXPERTS):
            @pl.when(used_ref[e] > 0)
            def _():
                tail_copy(e).start()
        for e in range(N_EXPERTS):
            @pl.when(used_ref[e] > 0)
            def _():
                tail_copy(e).wait()

    def row_copy(t, k):
        return pltpu.make_async_copy(h_ref.at[pl.ds(t, 1)], xs_hbm.at[pl.ds(row_ref[k, t], 1)], sem)

    def issue(i, carry):
        t0 = pl.multiple_of(i * ROW_UNROLL, ROW_UNROLL)
        for u in range(ROW_UNROLL):
            for k in range(TOP_K):
                row_copy(t0 + u, k).start()
        return carry

    def drain(i, carry):
        t0 = pl.multiple_of(i * ROW_UNROLL, ROW_UNROLL)
        for u in range(ROW_UNROLL):
            for k in range(TOP_K):
                row_copy(t0 + u, k).wait()
        return carry

    lax.fori_loop(0, tm // ROW_UNROLL, issue, 0)
    lax.fori_loop(0, tm // ROW_UNROLL, drain, 0)


def _dispatch(hp, xs_row, tables, counts, n_rows, tm=256):
    n_tok, dp = hp.shape
    grid_spec = pltpu.PrefetchScalarGridSpec(
        num_scalar_prefetch=2,
        grid=(n_tok // tm,),
        in_specs=[pl.BlockSpec((TOP_K, tm), lambda i, *_: (0, i), memory_space=pltpu.SMEM),
                  pl.BlockSpec((tm, dp), lambda i, *_: (i, 0))],
        out_specs=pl.BlockSpec(memory_space=pl.ANY),
        scratch_shapes=[pltpu.VMEM((SUB, dp), hp.dtype), pltpu.SemaphoreType.DMA, pltpu.SemaphoreType.DMA],
    )
    return pl.pallas_call(
        functools.partial(_dispatch_kernel, tm=tm),
        grid_spec=grid_spec,
        out_shape=jax.ShapeDtypeStruct((n_rows, dp), hp.dtype),
        compiler_params=_cparams(1),
        name="moe_dispatch",
    )(tables["xs_tail"], counts, xs_row, hp)


def _ffn_kernel(e_ref, blk0_ref, nsub_ref, act_ref, out_ref_idx, x0, x1, x2, x3, wg_ref, wu_ref, bg_ref, bu_ref,
                wd_ref, bd_ref, y_ref, xb, *, tf):
    del e_ref, blk0_ref, nsub_ref, out_ref_idx
    g = pl.program_id(0)
    j = pl.program_id(1)
    nact = act_ref[g]

    @pl.when(j == 0)
    def _():
        half = xb.shape[1] // 2
        for s, x_ref in enumerate((x0, x1, x2, x3)):
            @pl.when(s * (SUB // ROW_GRAN) < nact)
            def _():
                lo, hi = _unpack_pairs(x_ref[...])
                xb[s * SUB:(s + 1) * SUB, :half] = lo
                xb[s * SUB:(s + 1) * SUB, half:] = hi

    for n in range(1, SUPER // ROW_GRAN + 1):
        @pl.when(nact == n)
        def _():
            m = n * ROW_GRAN
            x = xb[0:m, :]
            gate = jnp.dot(x, wg_ref[...].astype(BF16), preferred_element_type=F32) + bg_ref[...]
            lin = jnp.dot(x, wu_ref[...].astype(BF16), preferred_element_type=F32) + bu_ref[...]
            gate = jnp.minimum(gate, SWIGLU_LIMIT)
            lin = jnp.clip(lin, -SWIGLU_LIMIT, SWIGLU_LIMIT)
            hid = (lin + 1.0) * gate * jax.nn.sigmoid(SWIGLU_ALPHA * gate)

            @pl.when(j == 0)
            def _():
                y_ref[0:m, :] = jnp.broadcast_to(bd_ref[...], (m, y_ref.shape[1]))

            y_ref[0:m, :] += jnp.dot(hid.astype(BF16), wd_ref[...].astype(BF16), preferred_element_type=F32)


def _expert_ffn(xs, tables, layer, w_gu, b_gu, w_down, b_down, n_super_max, tf=512):
    e_of, blk0, nsub, nsub_active, g_out = (tables[k] for k in ("e_of", "blk0", "nsub", "nsub_active", "g_out"))
    n_rows, dp = xs.shape
    d = 2 * dp
    n_we = w_gu.shape[0] * N_EXPERTS
    e_of = e_of + layer * N_EXPERTS
    w_gu = w_gu.reshape(n_we, d, 2 * D_FF)
    w_down = w_down.reshape(n_we, D_FF, d)
    b_gu = b_gu.reshape(n_we, 1, 2 * D_FF)
    b_down = b_down.reshape(n_we, 1, d)
    n_ff = D_FF // tf
    last = n_ff - 1

    def x_spec(s):
        return pl.BlockSpec(
            (SUB, dp), lambda g, j, e, b0, ns, na, go: (b0[g] + jnp.minimum(s, jnp.maximum(ns[g], 1) - 1), 0))

    def jj(j, na, g):
        return jnp.where(na[g] > 0, j, last)

    in_specs = [x_spec(s) for s in range(SUBS_PER_SUPER)] + [
        pl.BlockSpec((None, d, tf), lambda g, j, e, b0, ns, na, go: (e[g], 0, jj(j, na, g))),
        pl.BlockSpec((None, d, tf), lambda g, j, e, b0, ns, na, go: (e[g], 0, n_ff + jj(j, na, g))),
        pl.BlockSpec((None, 1, tf), lambda g, j, e, b0, ns, na, go: (e[g], 0, jj(j, na, g))),
        pl.BlockSpec((None, 1, tf), lambda g, j, e, b0, ns, na, go: (e[g], 0, n_ff + jj(j, na, g))),
        pl.BlockSpec((None, tf, d), lambda g, j, e, b0, ns, na, go: (e[g], jj(j, na, g), 0)),
        pl.BlockSpec((None, 1, d), lambda g, j, e, b0, ns, na, go: (e[g], 0, 0)),
    ]
    grid_spec = pltpu.PrefetchScalarGridSpec(
        num_scalar_prefetch=5,
        grid=(n_super_max, n_ff),
        in_specs=in_specs,
        out_specs=pl.BlockSpec((SUPER, d), lambda g, j, e, b0, ns, na, go: (go[g], 0)),
        scratch_shapes=[pltpu.VMEM((SUPER, d), BF16)],
    )
    return pl.pallas_call(
        functools.partial(_ffn_kernel, tf=tf),
        grid_spec=grid_spec,
        out_shape=jax.ShapeDtypeStruct((n_super_max * SUPER, d), F32),
        compiler_params=_cparams(2, FFN_VMEM_LIMIT),
        name="moe_ffn",
    )(e_of, blk0, nsub, nsub_active, g_out, xs, xs, xs, xs, w_gu, w_gu, b_gu, b_gu, w_down, b_down)


def _combine_kernel(row_ref, w_ref, h_ref, g_ref, b_ref, ys_hbm, of_ref, ob_ref, op_ref, ybuf, sem, *, tm):
    def row_copy(t, k):
        return pltpu.make_async_copy(ys_hbm.at[pl.ds(row_ref[k, t], 1)], ybuf.at[k, pl.ds(t, 1)], sem)

    def issue(i, carry):
        t0 = pl.multiple_of(i * ROW_UNROLL, ROW_UNROLL)
        for u in range(ROW_UNROLL):
            for k in range(TOP_K):
                row_copy(t0 + u, k).start()
        return carry

    def drain(i, carry):
        t0 = pl.multiple_of(i * ROW_UNROLL, ROW_UNROLL)
        for u in range(ROW_UNROLL):
            for k in range(TOP_K):
                row_copy(t0 + u, k).wait()
        return carry

    lax.fori_loop(0, tm // ROW_UNROLL, issue, 0)
    lax.fori_loop(0, tm // ROW_UNROLL, drain, 0)
    w = w_ref[...]
    ffn = w[:, 0:1] * ybuf[0]
    for k in range(1, TOP_K):
        ffn = ffn + w[:, k:k + 1] * ybuf[k]
    y = _layer_norm_rows(DEEPNORM_ALPHA * h_ref[...] + ffn, g_ref[...], b_ref[...])
    of_ref[...] = y
    ob_ref[...] = y.astype(ob_ref.dtype)
    op_ref[...] = _pack_pairs(y)


def _combine_ln(ys, ys_row, w_t, h, g, b, tm=128):
    n_tok, d = h.shape
    row = lambda i: (i, 0)
    const = lambda i: (0, 0)
    return pl.pallas_call(
        functools.partial(_combine_kernel, tm=tm),
        grid=(n_tok // tm,),
        in_specs=[pl.BlockSpec((TOP_K, tm), lambda i: (0, i), memory_space=pltpu.SMEM),
                  pl.BlockSpec((tm, TOP_K), row), pl.BlockSpec((tm, d), row),
                  pl.BlockSpec((1, d), const), pl.BlockSpec((1, d), const), pl.BlockSpec(memory_space=pl.ANY)],
        out_specs=[pl.BlockSpec((tm, d), row), pl.BlockSpec((tm, d), row), pl.BlockSpec((tm, d // 2), row)],
        out_shape=[jax.ShapeDtypeStruct((n_tok, d), F32), jax.ShapeDtypeStruct((n_tok, d), BF16),
                   jax.ShapeDtypeStruct((n_tok, d // 2), jnp.uint32)],
        scratch_shapes=[pltpu.VMEM((TOP_K, tm, d), F32), pltpu.SemaphoreType.DMA],
        compiler_params=_cparams(1),
        name="moe_combine_ln",
    )(ys_row, w_t, h, g.reshape(1, d), b.reshape(1, d), ys)


def _moe_ln(h, h_bf, h_pk, layer, w_router, b_router, w_gu, b_gu, w_down, b_down, ln_g, ln_b):
    n_tok, d = h.shape
    n_slots = n_tok * TOP_K
    top_idx, top_w, rank, cnt = _router(h_bf, w_router, b_router)
    counts = cnt[:, 0].astype(jnp.int32)
    tables = _moe_tables(counts, n_slots)
    xs_row, ys_row = _slot_rows(top_idx, rank, tables)
    n_super_max = N_EXPERTS + n_slots // SUPER
    n_rows = n_slots + N_EXPERTS * SUB
    xs = _dispatch(h_pk, xs_row, tables, counts, n_rows)
    ys = _expert_ffn(xs, tables, layer, w_gu, b_gu, w_down, b_down, n_super_max)
    return _combine_ln(ys, ys_row, top_w.T, h, ln_g, ln_b)


def _dilated_mix(x3, w_qkv, tabs):
    b, s, d = x3.shape
    n_tok = b * s
    width = A_HEADS * HEAD_DIM
    outs, lses = [], []
    for g, (window, dil) in enumerate(A_PATTERNS):
        assert window // dil == 128
        L = s // dil
        xp = x3.reshape(b, L, dil, d).transpose(0, 2, 1, 3).reshape(n_tok, d).astype(BF16)
        tabs_p = tabs.reshape(3, L, dil, HEAD_DIM).transpose(0, 2, 1, 3).reshape(3, s, HEAD_DIM)
        qkv = _proj(xp, w_qkv, tabs_p, col0=g * 3 * width, n_out=3 * width, pos_period=s, rope_mod=3, rope_cnt=2)
        o, lse = _band_attn(qkv.reshape(b * dil, L, 3 * width))
        outs.append(o.reshape(b, dil, L, width).transpose(0, 2, 1, 3).reshape(n_tok, width))
        lses.append(lse.reshape(b, dil, L, A_HEADS).transpose(0, 2, 1, 3).reshape(n_tok, A_HEADS))
    return _mix(outs, lses)


def kernel(x, a_w_qkv, a_w_o, kv_w, b_w_q, b_w_o, router_w, router_b, moe_w_gate_up, moe_b_gate_up, moe_w_down,
           moe_b_down, ln1_g, ln1_b, ln2_g, ln2_b):
    b, s, d = x.shape
    n_tok = b * s
    tabs = _rope_tables(s)
    h = x.reshape(n_tok, d)
    h_bf = h.astype(BF16)
    kv = None
    for layer in range(DEPTH):
        if layer < N_A_LAYERS:
            mix = _dilated_mix(h.reshape(b, s, d), a_w_qkv[layer], tabs)
            w_o = a_w_o[layer]
        else:
            j = layer - N_A_LAYERS
            if layer == N_A_LAYERS:
                kv = _proj(h_bf, kv_w, tabs, col0=0, n_out=2 * B_HEADS * HEAD_DIM, pos_period=s, rope_mod=2,
                           rope_cnt=1)
            q = _proj(h_bf, b_w_q[j], tabs, col0=0, n_out=B_HEADS * HEAD_DIM, pos_period=s, rope_mod=1, rope_cnt=1)
            mix = _moba(q.reshape(b, s, -1), kv.reshape(b, s, -1)).reshape(n_tok, -1)
            w_o = b_w_o[j]
        h, h_bf, h_pk = _oproj_ln(mix, w_o.astype(BF16), h, ln1_g[layer], ln1_b[layer])
        h, h_bf, _ = _moe_ln(h, h_bf, h_pk, layer, router_w[layer], router_b[layer], moe_w_gate_up, moe_b_gate_up,
                             moe_w_down, moe_b_down, ln2_g[layer], ln2_b[layer])
    return h.reshape(b, s, d)
```

```python
import functools

import jax
import jax.numpy as jnp
from jax import lax
from jax.experimental import pallas as pl
from jax.experimental.pallas import tpu as pltpu

D_MODEL = 2048
SEQ = 2048
DEPTH = 2
HEAD_DIM = 128
ROT_DIM = HEAD_DIM // 4
ROPE_THETA = 500000.0
A_PATTERNS = ((128, 1), (512, 4), (2048, 16))
A_GROUPS = len(A_PATTERNS)
A_HEADS = 16
B_HEADS = 16
MOBA_BLOCK = 256
MOBA_TOPK = 3
N_EXPERTS = 32
TOP_K = 4
D_FF = 2048
SWIGLU_LIMIT = 7.0
SWIGLU_ALPHA = 1.702
N_A_LAYERS = DEPTH // 2
DEEPNORM_ALPHA = (2 * DEPTH) ** 0.25
LN_EPS = 1e-5
NEG_INF = -1e30

LANES = 128
V7X_VMEM_BYTES = 64 * 1024 * 1024
VMEM_LIMIT = 56 * 1024 * 1024
FFN_VMEM_LIMIT = 61 * 1024 * 1024

BF16 = jnp.bfloat16
F32 = jnp.float32


def _cparams(n_axes, vmem_limit=VMEM_LIMIT):
    return pltpu.CompilerParams(dimension_semantics=("arbitrary",) * n_axes, vmem_limit_bytes=vmem_limit)


def _rope_tables(seq):
    half = ROT_DIM // 2
    inv = ROPE_THETA ** (-jnp.arange(0, ROT_DIM, 2, dtype=F32) / ROT_DIM)
    ang = jnp.arange(seq, dtype=F32)[:, None] * inv[None, :]
    cos, sin = jnp.cos(ang), jnp.sin(ang)
    zeros = jnp.zeros((seq, HEAD_DIM - ROT_DIM), F32)
    c = jnp.concatenate([cos, cos, jnp.ones((seq, HEAD_DIM - ROT_DIM), F32)], axis=1)
    s1 = jnp.concatenate([-sin, jnp.zeros((seq, half), F32), zeros], axis=1)
    s2 = jnp.concatenate([jnp.zeros((seq, half), F32), sin, zeros], axis=1)
    return jnp.stack([c, s1, s2])


def _rope_tile(x, c, s1, s2):
    half = ROT_DIM // 2
    return x * c + pltpu.roll(x, HEAD_DIM - half, 1) * s1 + pltpu.roll(x, half, 1) * s2


def _proj_kernel(a_ref, w_ref, tab_ref, o_ref, acc_s, *, tn):
    @pl.when(pl.program_id(0) == 0)
    def _():
        acc_s[...] = jnp.zeros_like(acc_s)

    prev = acc_s[...]
    c, s1, s2 = tab_ref[0], tab_ref[1], tab_ref[2]
    for h in range(tn // HEAD_DIM):
        sl = slice(h * HEAD_DIM, (h + 1) * HEAD_DIM)
        o_ref[:, sl] = _rope_tile(prev[:, sl], c, s1, s2).astype(o_ref.dtype)
    acc_s[...] = jnp.dot(a_ref[...], w_ref[...].astype(BF16), preferred_element_type=F32)


def _proj(a, w, tabs, *, col0, n_out, pos_period, rope_mod, rope_cnt, tm=1024, tn=512):
    m, k = a.shape
    assert m % tm == 0 and n_out % tn == 0 and col0 % tn == 0 and pos_period % tm == 0 and D_MODEL % tn == 0
    jb = col0 // tn
    per = pos_period // tm
    ident = jnp.stack([jnp.ones_like(tabs[0]), jnp.zeros_like(tabs[0]), jnp.zeros_like(tabs[0])])
    tabs2 = jnp.stack([tabs, ident])

    nj = n_out // tn
    n_tiles = (m // tm) * nj

    def plain(j):
        return jnp.where(((j * tn) // D_MODEL) % rope_mod < rope_cnt, 0, 1)

    cur = lambda t: jnp.minimum(t, n_tiles - 1)
    fin = lambda t: jnp.maximum(t - 1, 0)
    return pl.pallas_call(
        functools.partial(_proj_kernel, tn=tn),
        grid=(n_tiles + 1,),
        in_specs=[
            pl.BlockSpec((tm, k), lambda t: (cur(t) // nj, 0)),
            pl.BlockSpec((k, tn), lambda t: (0, cur(t) % nj + jb)),
            pl.BlockSpec((None, 3, tm, HEAD_DIM), lambda t: (plain(fin(t) % nj), 0, (fin(t) // nj) % per, 0)),
        ],
        out_specs=pl.BlockSpec((tm, tn), lambda t: (fin(t) // nj, fin(t) % nj)),
        out_shape=jax.ShapeDtypeStruct((m, n_out), BF16),
        scratch_shapes=[pltpu.VMEM((tm, tn), F32)],
        compiler_params=_cparams(1),
        name="proj_rope",
    )(a, w, tabs2)


def _band_attn_kernel(*refs, has_prev, n_heads, blk):
    if has_prev:
        q_ref, kp_ref, kc_ref, vp_ref, vc_ref, o_ref, lse_ref = refs
    else:
        q_ref, kc_ref, vc_ref, o_ref, lse_ref = refs
    i = pl.program_id(1)
    nk = 2 * blk if has_prev else blk
    qi = lax.broadcasted_iota(jnp.int32, (blk, nk), 0)
    ki = lax.broadcasted_iota(jnp.int32, (blk, nk), 1) - (nk - blk)
    diff = qi - ki
    mask = (diff >= 0) & (diff <= blk)
    if has_prev:
        mask = mask & ((i > 0) | (ki >= 0))
    scale = HEAD_DIM ** -0.5
    lane = lax.broadcasted_iota(jnp.int32, (blk, n_heads), 1)
    lse_all = jnp.zeros((blk, n_heads), F32)
    for h in range(n_heads):
        sl = slice(h * HEAD_DIM, (h + 1) * HEAD_DIM)
        q = q_ref[:, sl]
        if has_prev:
            k = jnp.concatenate([kp_ref[:, sl], kc_ref[:, sl]], axis=0)
            v = jnp.concatenate([vp_ref[:, sl], vc_ref[:, sl]], axis=0)
        else:
            k, v = kc_ref[:, sl], vc_ref[:, sl]
        s = lax.dot_general(q, k, (((1,), (1,)), ((), ())), preferred_element_type=F32) * scale
        s = jnp.where(mask, s, NEG_INF)
        m = jnp.max(s, axis=1, keepdims=True)
        p = jnp.exp(s - m)
        den = jnp.sum(p, axis=1, keepdims=True)
        o = jnp.dot(p.astype(BF16), v, preferred_element_type=F32) / den
        o_ref[:, sl] = o.astype(o_ref.dtype)
        lse_all = jnp.where(lane == h, m + jnp.log(den), lse_all)
    lse_ref[...] = lse_all


def _band_attn(qkv, n_heads=A_HEADS, blk=128):
    n, L, c3 = qkv.shape
    c = c3 // 3
    assert c == n_heads * HEAD_DIM and L % blk == 0
    nb = L // blk
    has_prev = True
    spec = lambda col, prev: pl.BlockSpec(
        (None, blk, c), (lambda s, i: (s, jnp.maximum(i - 1, 0), col)) if prev else (lambda s, i: (s, i, col)))
    if has_prev:
        in_specs = [spec(0, False), spec(1, True), spec(1, False), spec(2, True), spec(2, False)]
        args = (qkv,) * 5
    else:
        in_specs = [spec(0, False), spec(1, False), spec(2, False)]
        args = (qkv,) * 3
    kern = functools.partial(_band_attn_kernel, has_prev=has_prev, n_heads=n_heads, blk=blk)
    return pl.pallas_call(
        kern,
        grid=(n, nb),
        in_specs=in_specs,
        out_specs=[pl.BlockSpec((None, blk, c), lambda s, i: (s, i, 0)),
                   pl.BlockSpec((None, blk, n_heads), lambda s, i: (s, i, 0))],
        out_shape=[jax.ShapeDtypeStruct((n, L, c), BF16), jax.ShapeDtypeStruct((n, L, n_heads), F32)],
        compiler_params=_cparams(2),
        name="band_attn",
    )(*args)


def _mix_kernel(o0, o1, o2, l0, l1, l2, out_ref, *, n_heads):
    ls = [l0[...], l1[...], l2[...]]
    mx = jnp.maximum(jnp.maximum(ls[0], ls[1]), ls[2])
    es = [jnp.exp(l - mx) for l in ls]
    tot = es[0] + es[1] + es[2]
    ws = [e / tot for e in es]
    os_ = [o0, o1, o2]
    for h in range(n_heads):
        sl = slice(h * HEAD_DIM, (h + 1) * HEAD_DIM)
        acc = ws[0][:, h:h + 1] * os_[0][:, sl].astype(F32)
        acc += ws[1][:, h:h + 1] * os_[1][:, sl].astype(F32)
        acc += ws[2][:, h:h + 1] * os_[2][:, sl].astype(F32)
        out_ref[:, sl] = acc.astype(out_ref.dtype)


def _mix(os_, ls, tm=256, n_heads=A_HEADS):
    m, c = os_[0].shape
    ospec = pl.BlockSpec((tm, c), lambda i: (i, 0))
    lspec = pl.BlockSpec((tm, n_heads), lambda i: (i, 0))
    return pl.pallas_call(
        functools.partial(_mix_kernel, n_heads=n_heads),
        grid=(m // tm,),
        in_specs=[ospec] * 3 + [lspec] * 3,
        out_specs=ospec,
        out_shape=jax.ShapeDtypeStruct((m, c), BF16),
        compiler_params=_cparams(1),
        name="group_mix",
    )(*os_, *ls)


def _layer_norm_rows(y, g, b):
    mu = jnp.mean(y, axis=1, keepdims=True)
    yc = y - mu
    var = jnp.mean(yc * yc, axis=1, keepdims=True)
    return yc * lax.rsqrt(var + LN_EPS) * g + b


def _pack_pairs(y):
    half = y.shape[1] // 2
    lo = lax.bitcast_convert_type(y[:, :half].astype(BF16).astype(F32), jnp.uint32)
    hi = lax.bitcast_convert_type(y[:, half:].astype(BF16).astype(F32), jnp.uint32)
    return hi | (lo >> 16)


def _unpack_pairs(u):
    lo = lax.bitcast_convert_type(u << 16, F32).astype(BF16)
    hi = lax.bitcast_convert_type(u & jnp.uint32(0xFFFF0000), F32).astype(BF16)
    return lo, hi


def _oproj_ln_kernel(a_ref, w_ref, h_ref, g_ref, b_ref, of_ref, ob_ref, op_ref):
    acc = jnp.dot(a_ref[...], w_ref[...], preferred_element_type=F32)
    y = _layer_norm_rows(DEEPNORM_ALPHA * h_ref[...] + acc, g_ref[...], b_ref[...])
    of_ref[...] = y
    ob_ref[...] = y.astype(ob_ref.dtype)
    op_ref[...] = _pack_pairs(y)


def _oproj_ln(a, w_bf, h, g, b, tm=512):
    m, k = a.shape
    d = w_bf.shape[1]
    row = lambda i: (i, 0)
    const = lambda i: (0, 0)
    return pl.pallas_call(
        _oproj_ln_kernel,
        grid=(m // tm,),
        in_specs=[pl.BlockSpec((tm, k), row), pl.BlockSpec((k, d), const), pl.BlockSpec((tm, d), row),
                  pl.BlockSpec((1, d), const), pl.BlockSpec((1, d), const)],
        out_specs=[pl.BlockSpec((tm, d), row), pl.BlockSpec((tm, d), row), pl.BlockSpec((tm, d // 2), row)],
        out_shape=[jax.ShapeDtypeStruct((m, d), F32), jax.ShapeDtypeStruct((m, d), BF16),
                   jax.ShapeDtypeStruct((m, d // 2), jnp.uint32)],
        compiler_params=_cparams(1),
        name="oproj_ln",
    )(a, w_bf, h, g.reshape(1, d), b.reshape(1, d))


def _moba_kernel(q_ref, k_ref, v_ref, et_ref, o_ref, km_s, *, heads, seq, blk):
    c = pl.program_id(2)
    nblk = seq // blk
    hw = heads * HEAD_DIM
    scale = HEAD_DIM ** -0.5

    @pl.when(c == 0)
    def _():
        ar = lax.broadcasted_iota(jnp.int32, (16, seq), 0)
        ac = lax.broadcasted_iota(jnp.int32, (16, seq), 1)
        avg = jnp.where(ac // blk == ar, 1.0 / blk, 0.0).astype(BF16)
        kmean = jnp.dot(avg, k_ref[...], preferred_element_type=F32)[0:nblk]
        kmt = jnp.concatenate([kmean] * (LANES // nblk), axis=0)
        kr = lax.broadcasted_iota(jnp.int32, (LANES, hw), 0)
        kc = lax.broadcasted_iota(jnp.int32, (LANES, hw), 1)
        km_s[...] = jnp.where(kr // nblk == kc // HEAD_DIM, kmt, 0.0).astype(BF16)

    gate = lax.dot_general(q_ref[...], km_s[...], (((1,), (1,)), ((), ())),
                           preferred_element_type=F32)
    lane = lax.broadcasted_iota(jnp.int32, (blk, LANES), 1)
    n = lane % nblk
    valid = (n < c) & (lane < heads * nblk)
    g = jnp.where(valid, gate, NEG_INF)
    cnt = jnp.zeros((blk, LANES), jnp.int32)
    for sh in range(1, nblk):
        lo = pltpu.roll(g, sh, 1)
        cnt = cnt + jnp.where((n >= sh) & (lo >= g), 1, 0)
        hi = pltpu.roll(g, LANES - sh, 1)
        cnt = cnt + jnp.where((n + sh < nblk) & (hi > g), 1, 0)
    sel = (cnt < MOBA_TOPK) & valid
    bias = jnp.where(sel | (n == c), 0.0, NEG_INF).astype(F32)

    for nb in range(2, nblk + 1, 2):
        @pl.when(2 * (c // 2 + 1) == nb)
        def _():
            kw = nb * blk
            past = kw - 2 * blk
            qpos = c * blk + lax.broadcasted_iota(jnp.int32, (blk, 2 * blk), 0)
            kpos = past + lax.broadcasted_iota(jnp.int32, (blk, 2 * blk), 1)
            causal = kpos <= qpos
            et = et_ref[0:kw, :]
            for h in range(heads):
                sl = slice(h * HEAD_DIM, (h + 1) * HEAD_DIM)
                bias_h = bias if h == 0 else pltpu.roll(bias, LANES - h * nblk, 1)
                q_aug = jnp.concatenate([q_ref[:, sl], bias_h.astype(BF16)], axis=1)
                k_aug = jnp.concatenate([k_ref[0:kw, sl], et], axis=1)
                s = lax.dot_general(q_aug, k_aug, (((1,), (1,)), ((), ())), preferred_element_type=F32) * scale
                tail = jnp.where(causal, s[:, past:], NEG_INF)
                s = tail if past == 0 else jnp.concatenate([s[:, :past], tail], axis=1)
                m = jnp.max(s, axis=1, keepdims=True)
                p = jnp.exp(s - m)
                den = jnp.sum(p, axis=1, keepdims=True)
                o = jnp.dot(p.astype(BF16), v_ref[0:kw, sl], preferred_element_type=F32) / den
                o_ref[:, sl] = o.astype(o_ref.dtype)


def _moba(q, kv, heads_per_step=4, blk=MOBA_BLOCK):
    b, s, c = q.shape
    n_heads = c // HEAD_DIM
    nblk = s // blk
    assert nblk % 2 == 0 and LANES % nblk == 0 and heads_per_step * nblk <= LANES
    hw = heads_per_step * HEAD_DIM
    ng = n_heads // heads_per_step
    et = (jnp.arange(s, dtype=jnp.int32)[:, None] // blk == jnp.arange(LANES, dtype=jnp.int32)[None, :]).astype(BF16)
    kern = functools.partial(_moba_kernel, heads=heads_per_step, seq=s, blk=blk)
    return pl.pallas_call(
        kern,
        grid=(b, ng, s // blk),
        in_specs=[pl.BlockSpec((None, blk, hw), lambda bi, g, t: (bi, t, g)),
                  pl.BlockSpec((None, s, hw), lambda bi, g, t: (bi, 0, g)),
                  pl.BlockSpec((None, s, hw), lambda bi, g, t: (bi, 0, g + ng)),
                  pl.BlockSpec((s, LANES), lambda bi, g, t: (0, 0))],
        out_specs=pl.BlockSpec((None, blk, hw), lambda bi, g, t: (bi, t, g)),
        out_shape=jax.ShapeDtypeStruct((b, s, c), BF16),
        scratch_shapes=[pltpu.VMEM((LANES, hw), BF16)],
        compiler_params=_cparams(3),
        name="moba_attn",
    )(q, kv, kv, et)


def _router_kernel(h_ref, wr_ref, br_ref, idx_ref, w_ref, rank_ref, cnt_ref, *, tm):
    t = pl.program_id(0)

    @pl.when(t == 0)
    def _():
        cnt_ref[...] = jnp.zeros_like(cnt_ref)

    logits = lax.dot_general(wr_ref[...], h_ref[...], (((1,), (1,)), ((), ())),
                             preferred_element_type=F32) + br_ref[...]
    row = lax.broadcasted_iota(jnp.int32, (N_EXPERTS, tm), 0)
    rem = logits
    vals, idxs, hots = [], [], []
    for _ in range(TOP_K):
        mx = jnp.max(rem, axis=0, keepdims=True)
        ix = jnp.min(jnp.where(rem == mx, row, N_EXPERTS), axis=0, keepdims=True)
        hot = row == ix
        vals.append(mx)
        idxs.append(ix)
        hots.append(hot)
        rem = jnp.where(hot, -jnp.inf, rem)
    es = [jnp.exp(v - vals[0]) for v in vals]
    tot = es[0] + es[1] + es[2] + es[3]
    sel = jnp.zeros((N_EXPERTS, tm), F32)
    for hot in hots:
        sel = sel + hot.astype(F32)
    ri = lax.broadcasted_iota(jnp.int32, (tm, tm), 0)
    ci = lax.broadcasted_iota(jnp.int32, (tm, tm), 1)
    upper = jnp.where(ri <= ci, 1.0, 0.0).astype(BF16)
    incl = jnp.dot(sel.astype(BF16), upper, preferred_element_type=F32)
    base = cnt_ref[:, 0:1]
    rank_e = base + incl - sel
    ranks = [jnp.sum(jnp.where(hot, rank_e, 0.0), axis=0, keepdims=True) for hot in hots]
    idx_ref[...] = jnp.concatenate(idxs, axis=0)
    w_ref[...] = jnp.concatenate([e / tot for e in es], axis=0)
    rank_ref[...] = jnp.concatenate(ranks, axis=0).astype(jnp.int32)
    cnt_ref[...] = jnp.broadcast_to(base + incl[:, tm - 1:tm], cnt_ref.shape)


def _router(h_bf, w_router, b_router, tm=512):
    n_tok, d = h_bf.shape
    wr_t = w_router.T.astype(BF16)
    br = b_router.reshape(N_EXPERTS, 1).astype(F32)
    tok = lambda i: (0, i)
    const = lambda i: (0, 0)
    return pl.pallas_call(
        functools.partial(_router_kernel, tm=tm),
        grid=(n_tok // tm,),
        in_specs=[pl.BlockSpec((tm, d), lambda i: (i, 0)), pl.BlockSpec((N_EXPERTS, d), const),
                  pl.BlockSpec((N_EXPERTS, 1), const)],
        out_specs=[pl.BlockSpec((TOP_K, tm), tok), pl.BlockSpec((TOP_K, tm), tok), pl.BlockSpec((TOP_K, tm), tok),
                   pl.BlockSpec((N_EXPERTS, LANES), const)],
        out_shape=[jax.ShapeDtypeStruct((TOP_K, n_tok), jnp.int32), jax.ShapeDtypeStruct((TOP_K, n_tok), F32),
                   jax.ShapeDtypeStruct((TOP_K, n_tok), jnp.int32), jax.ShapeDtypeStruct((N_EXPERTS, LANES), F32)],
        compiler_params=_cparams(1),
        name="moe_router",
    )(h_bf, wr_t, br)


SUB = 128
SUPER = 1152
SUBS_PER_SUPER = SUPER // SUB


ROW_UNROLL = 8
ROW_GRAN = SUB


def _moe_tables(counts, n_slots):
    n_super_max = N_EXPERTS + n_slots // SUPER
    n_sub_e = (counts + SUB - 1) // SUB
    xs_start = SUB * (jnp.cumsum(n_sub_e) - n_sub_e)
    n_sb_e = (n_sub_e + SUBS_PER_SUPER - 1) // SUBS_PER_SUPER
    per_e = jnp.maximum((n_sub_e + jnp.maximum(n_sb_e, 1) - 1) // jnp.maximum(n_sb_e, 1), 1)
    sb_end = jnp.cumsum(n_sb_e)
    sb_start = sb_end - n_sb_e
    total = sb_end[-1]
    g = jnp.arange(n_super_max, dtype=jnp.int32)
    gc = jnp.minimum(g, total - 1)
    e_of = jnp.minimum(jnp.searchsorted(sb_end, gc, side="right"), N_EXPERTS - 1).astype(jnp.int32)
    j_in = gc - sb_start[e_of]
    blk0 = xs_start[e_of] // SUB + per_e[e_of] * j_in
    nsub = jnp.clip(n_sub_e[e_of] - per_e[e_of] * j_in, 0, per_e[e_of])
    rows = jnp.clip(counts[e_of] - SUB * per_e[e_of] * j_in, 0, SUB * nsub)
    nsub_active = jnp.where(g < total, (rows + ROW_GRAN - 1) // ROW_GRAN, 0)
    ys_start = SUPER * sb_start
    xs_tail = xs_start + SUB * jnp.maximum(n_sub_e - 1, 0)
    i32 = lambda a: a.astype(jnp.int32)
    return dict(xs_start=i32(xs_start), ys_start=i32(ys_start), per=i32(per_e), xs_tail=i32(xs_tail),
                e_of=i32(e_of), blk0=i32(blk0), nsub=i32(nsub), nsub_active=i32(nsub_active), g_out=i32(gc))


def _slot_rows_kernel(xs_start_ref, ys_start_ref, per_ref, idx_ref, rank_ref, xr_ref, yr_ref):
    idx = idx_ref[...]
    rank = rank_ref[...]
    xs0 = jnp.zeros_like(idx)
    ys0 = jnp.zeros_like(idx)
    per = jnp.ones_like(idx)
    for e in range(N_EXPERTS):
        hit = idx == e
        xs0 = jnp.where(hit, xs_start_ref[e], xs0)
        ys0 = jnp.where(hit, ys_start_ref[e], ys0)
        per = jnp.where(hit, per_ref[e], per)
    sub = jnp.right_shift(rank, SUB.bit_length() - 1)
    q = jnp.floor((sub.astype(F32) + 0.5) / per.astype(F32)).astype(jnp.int32)
    xr_ref[...] = xs0 + rank
    yr_ref[...] = ys0 + q * SUPER + (rank - q * per * SUB)


def _slot_rows(top_idx, rank, tables, tm=2048):
    k, n_tok = top_idx.shape
    tm = min(tm, n_tok)
    blk = pl.BlockSpec((k, tm), lambda i, *_: (0, i))
    grid_spec = pltpu.PrefetchScalarGridSpec(
        num_scalar_prefetch=3, grid=(n_tok // tm,), in_specs=[blk, blk], out_specs=[blk, blk])
    return pl.pallas_call(
        _slot_rows_kernel,
        grid_spec=grid_spec,
        out_shape=[jax.ShapeDtypeStruct((k, n_tok), jnp.int32)] * 2,
        compiler_params=_cparams(1),
        name="moe_slot_rows",
    )(tables["xs_start"], tables["ys_start"], tables["per"], top_idx, rank)


def _dispatch_kernel(tail_ref, used_ref, row_ref, h_ref, xs_hbm, zbuf, sem, zsem, *, tm):
    def tail_copy(e):
        return pltpu.make_async_copy(zbuf, xs_hbm.at[pl.ds(pl.multiple_of(tail_ref[e], SUB), SUB)], zsem)

    @pl.when(pl.program_id(0) == 0)
    def _():
        zbuf[...] = jnp.zeros_like(zbuf)
        for e in range(N_EXPERTS):
            @pl.when(used_ref[e] > 0)
            def _():
                tail_copy(e).start()
        for e in range(N_EXPERTS):
            @pl.when(used_ref[e] > 0)
            def _():
                tail_copy(e).wait()

    def row_copy(t, k):
        return pltpu.make_async_copy(h_ref.at[pl.ds(t, 1)], xs_hbm.at[pl.ds(row_ref[k, t], 1)], sem)

    def issue(i, carry):
        t0 = pl.multiple_of(i * ROW_UNROLL, ROW_UNROLL)
        for u in range(ROW_UNROLL):
            for k in range(TOP_K):
                row_copy(t0 + u, k).start()
        return carry

    def drain(i, carry):
        t0 = pl.multiple_of(i * ROW_UNROLL, ROW_UNROLL)
        for u in range(ROW_UNROLL):
            for k in range(TOP_K):
                row_copy(t0 + u, k).wait()
        return carry

    lax.fori_loop(0, tm // ROW_UNROLL, issue, 0)
    lax.fori_loop(0, tm // ROW_UNROLL, drain, 0)


def _dispatch(hp, xs_row, tables, counts, n_rows, tm=256):
    n_tok, dp = hp.shape
    grid_spec = pltpu.PrefetchScalarGridSpec(
        num_scalar_prefetch=2,
        grid=(n_tok // tm,),
        in_specs=[pl.BlockSpec((TOP_K, tm), lambda i, *_: (0, i), memory_space=pltpu.SMEM),
                  pl.BlockSpec((tm, dp), lambda i, *_: (i, 0))],
        out_specs=pl.BlockSpec(memory_space=pl.ANY),
        scratch_shapes=[pltpu.VMEM((SUB, dp), hp.dtype), pltpu.SemaphoreType.DMA, pltpu.SemaphoreType.DMA],
    )
    return pl.pallas_call(
        functools.partial(_dispatch_kernel, tm=tm),
        grid_spec=grid_spec,
        out_shape=jax.ShapeDtypeStruct((n_rows, dp), hp.dtype),
        compiler_params=_cparams(1),
        name="moe_dispatch",
    )(tables["xs_tail"], counts, xs_row, hp)


def _ffn_kernel(e_ref, blk0_ref, nsub_ref, act_ref, out_ref_idx, *refs, tf):
    del e_ref, blk0_ref, nsub_ref, out_ref_idx
    x_refs = refs[:SUBS_PER_SUPER]
    wg_ref, wu_ref, bg_ref, bu_ref, wd_ref, bd_ref, y_ref = refs[SUBS_PER_SUPER:]
    g = pl.program_id(0)
    j = pl.program_id(1)
    nact = act_ref[g]
    half = x_refs[0].shape[1]

    for n in range(1, SUBS_PER_SUPER + 1):
        @pl.when(nact == n)
        def _():
            m = n * SUB
            parts = [_unpack_pairs(x_refs[s][...]) for s in range(n)]
            x_lo = jnp.concatenate([p[0] for p in parts], axis=0)
            x_hi = jnp.concatenate([p[1] for p in parts], axis=0)

            def xdot(w_ref):
                return (jnp.dot(x_lo, w_ref[0:half, :].astype(BF16), preferred_element_type=F32)
                        + jnp.dot(x_hi, w_ref[half:, :].astype(BF16), preferred_element_type=F32))

            gate = xdot(wg_ref) + bg_ref[...]
            lin = xdot(wu_ref) + bu_ref[...]
            gate = jnp.minimum(gate, SWIGLU_LIMIT)
            lin = jnp.clip(lin, -SWIGLU_LIMIT, SWIGLU_LIMIT)
            hid = (lin + 1.0) * gate * jax.nn.sigmoid(SWIGLU_ALPHA * gate)

            @pl.when(j == 0)
            def _():
                y_ref[0:m, :] = jnp.broadcast_to(bd_ref[...], (m, y_ref.shape[1]))

            y_ref[0:m, :] += jnp.dot(hid.astype(BF16), wd_ref[...].astype(BF16), preferred_element_type=F32)


def _expert_ffn(xs, tables, layer, w_gu, b_gu, w_down, b_down, n_super_max, tf=512):
    e_of, blk0, nsub, nsub_active, g_out = (tables[k] for k in ("e_of", "blk0", "nsub", "nsub_active", "g_out"))
    n_rows, dp = xs.shape
    d = 2 * dp
    n_we = w_gu.shape[0] * N_EXPERTS
    e_of = e_of + layer * N_EXPERTS
    w_gu = w_gu.reshape(n_we, d, 2 * D_FF)
    w_down = w_down.reshape(n_we, D_FF, d)
    b_gu = b_gu.reshape(n_we, 1, 2 * D_FF)
    b_down = b_down.reshape(n_we, 1, d)
    n_ff = D_FF // tf
    last = n_ff - 1

    def x_spec(s):
        return pl.BlockSpec(
            (SUB, dp), lambda g, j, e, b0, ns, na, go: (b0[g] + jnp.minimum(s, jnp.maximum(ns[g], 1) - 1), 0))

    def jj(j, na, g):
        return jnp.where(na[g] > 0, j, last)

    in_specs = [x_spec(s) for s in range(SUBS_PER_SUPER)] + [
        pl.BlockSpec((None, d, tf), lambda g, j, e, b0, ns, na, go: (e[g], 0, jj(j, na, g))),
        pl.BlockSpec((None, d, tf), lambda g, j, e, b0, ns, na, go: (e[g], 0, n_ff + jj(j, na, g))),
        pl.BlockSpec((None, 1, tf), lambda g, j, e, b0, ns, na, go: (e[g], 0, jj(j, na, g))),
        pl.BlockSpec((None, 1, tf), lambda g, j, e, b0, ns, na, go: (e[g], 0, n_ff + jj(j, na, g))),
        pl.BlockSpec((None, tf, d), lambda g, j, e, b0, ns, na, go: (e[g], jj(j, na, g), 0)),
        pl.BlockSpec((None, 1, d), lambda g, j, e, b0, ns, na, go: (e[g], 0, 0)),
    ]
    grid_spec = pltpu.PrefetchScalarGridSpec(
        num_scalar_prefetch=5,
        grid=(n_super_max, n_ff),
        in_specs=in_specs,
        out_specs=pl.BlockSpec((SUPER, d), lambda g, j, e, b0, ns, na, go: (go[g], 0)),
    )
    return pl.pallas_call(
        functools.partial(_ffn_kernel, tf=tf),
        grid_spec=grid_spec,
        out_shape=jax.ShapeDtypeStruct((n_super_max * SUPER, d), F32),
        compiler_params=_cparams(2, FFN_VMEM_LIMIT),
        name="moe_ffn",
    )(e_of, blk0, nsub, nsub_active, g_out, *([xs] * SUBS_PER_SUPER), w_gu, w_gu, b_gu, b_gu, w_down, b_down)


def _combine_kernel(row_ref, w_ref, h_ref, g_ref, b_ref, ys_hbm, of_ref, ob_ref, op_ref, ybuf, sem, *, tm):
    def row_copy(t, k):
        return pltpu.make_async_copy(ys_hbm.at[pl.ds(row_ref[k, t], 1)], ybuf.at[k, pl.ds(t, 1)], sem)

    def issue(i, carry):
        t0 = pl.multiple_of(i * ROW_UNROLL, ROW_UNROLL)
        for u in range(ROW_UNROLL):
            for k in range(TOP_K):
                row_copy(t0 + u, k).start()
        return carry

    def drain(i, carry):
        t0 = pl.multiple_of(i * ROW_UNROLL, ROW_UNROLL)
        for u in range(ROW_UNROLL):
            for k in range(TOP_K):
                row_copy(t0 + u, k).wait()
        return carry

    lax.fori_loop(0, tm // ROW_UNROLL, issue, 0)
    lax.fori_loop(0, tm // ROW_UNROLL, drain, 0)
    w = w_ref[...]
    ffn = w[:, 0:1] * ybuf[0]
    for k in range(1, TOP_K):
        ffn = ffn + w[:, k:k + 1] * ybuf[k]
    y = _layer_norm_rows(DEEPNORM_ALPHA * h_ref[...] + ffn, g_ref[...], b_ref[...])
    of_ref[...] = y
    ob_ref[...] = y.astype(ob_ref.dtype)
    op_ref[...] = _pack_pairs(y)


def _combine_ln(ys, ys_row, w_t, h, g, b, tm=128):
    n_tok, d = h.shape
    row = lambda i: (i, 0)
    const = lambda i: (0, 0)
    return pl.pallas_call(
        functools.partial(_combine_kernel, tm=tm),
        grid=(n_tok // tm,),
        in_specs=[pl.BlockSpec((TOP_K, tm), lambda i: (0, i), memory_space=pltpu.SMEM),
                  pl.BlockSpec((tm, TOP_K), row), pl.BlockSpec((tm, d), row),
                  pl.BlockSpec((1, d), const), pl.BlockSpec((1, d), const), pl.BlockSpec(memory_space=pl.ANY)],
        out_specs=[pl.BlockSpec((tm, d), row), pl.BlockSpec((tm, d), row), pl.BlockSpec((tm, d // 2), row)],
        out_shape=[jax.ShapeDtypeStruct((n_tok, d), F32), jax.ShapeDtypeStruct((n_tok, d), BF16),
                   jax.ShapeDtypeStruct((n_tok, d // 2), jnp.uint32)],
        scratch_shapes=[pltpu.VMEM((TOP_K, tm, d), F32), pltpu.SemaphoreType.DMA],
        compiler_params=_cparams(1),
        name="moe_combine_ln",
    )(ys_row, w_t, h, g.reshape(1, d), b.reshape(1, d), ys)


def _moe_ln(h, h_bf, h_pk, layer, w_router, b_router, w_gu, b_gu, w_down, b_down, ln_g, ln_b):
    n_tok, d = h.shape
    n_slots = n_tok * TOP_K
    top_idx, top_w, rank, cnt = _router(h_bf, w_router, b_router)
    counts = cnt[:, 0].astype(jnp.int32)
    tables = _moe_tables(counts, n_slots)
    xs_row, ys_row = _slot_rows(top_idx, rank, tables)
    n_super_max = N_EXPERTS + n_slots // SUPER
    n_rows = n_slots + N_EXPERTS * SUB
    xs = _dispatch(h_pk, xs_row, tables, counts, n_rows)
    ys = _expert_ffn(xs, tables, layer, w_gu, b_gu, w_down, b_down, n_super_max)
    return _combine_ln(ys, ys_row, top_w.T, h, ln_g, ln_b)


def _dilated_mix(x3, w_qkv, tabs):
    b, s, d = x3.shape
    n_tok = b * s
    width = A_HEADS * HEAD_DIM
    outs, lses = [], []
    for g, (window, dil) in enumerate(A_PATTERNS):
        assert window // dil == 128
        L = s // dil
        xp = x3.reshape(b, L, dil, d).transpose(0, 2, 1, 3).reshape(n_tok, d).astype(BF16)
        tabs_p = tabs.reshape(3, L, dil, HEAD_DIM).transpose(0, 2, 1, 3).reshape(3, s, HEAD_DIM)
        qkv = _proj(xp, w_qkv, tabs_p, col0=g * 3 * width, n_out=3 * width, pos_period=s, rope_mod=3, rope_cnt=2)
        o, lse = _band_attn(qkv.reshape(b * dil, L, 3 * width))
        outs.append(o.reshape(b, dil, L, width).transpose(0, 2, 1, 3).reshape(n_tok, width))
        lses.append(lse.reshape(b, dil, L, A_HEADS).transpose(0, 2, 1, 3).reshape(n_tok, A_HEADS))
    return _mix(outs, lses)


def kernel(x, a_w_qkv, a_w_o, kv_w, b_w_q, b_w_o, router_w, router_b, moe_w_gate_up, moe_b_gate_up, moe_w_down,
           moe_b_down, ln1_g, ln1_b, ln2_g, ln2_b):
    b, s, d = x.shape
    n_tok = b * s
    tabs = _rope_tables(s)
    h = x.reshape(n_tok, d)
    h_bf = h.astype(BF16)
    kv = None
    for layer in range(DEPTH):
        if layer < N_A_LAYERS:
            mix = _dilated_mix(h.reshape(b, s, d), a_w_qkv[layer], tabs)
            w_o = a_w_o[layer]
        else:
            j = layer - N_A_LAYERS
            if layer == N_A_LAYERS:
                kv = _proj(h_bf, kv_w, tabs, col0=0, n_out=2 * B_HEADS * HEAD_DIM, pos_period=s, rope_mod=2,
                           rope_cnt=1)
            q = _proj(h_bf, b_w_q[j], tabs, col0=0, n_out=B_HEADS * HEAD_DIM, pos_period=s, rope_mod=1, rope_cnt=1)
            mix = _moba(q.reshape(b, s, -1), kv.reshape(b, s, -1)).reshape(n_tok, -1)
            w_o = b_w_o[j]
        h, h_bf, h_pk = _oproj_ln(mix, w_o.astype(BF16), h, ln1_g[layer], ln1_b[layer])
        h, h_bf, _ = _moe_ln(h, h_bf, h_pk, layer, router_w[layer], router_b[layer], moe_w_gate_up, moe_b_gate_up,
                             moe_w_down, moe_b_down, ln2_g[layer], ln2_b[layer])
    return h.reshape(b, s, d)
```

```python
import functools

import jax
import jax.numpy as jnp
from jax import lax
from jax.experimental import pallas as pl
from jax.experimental.pallas import tpu as pltpu

D_MODEL = 2048
SEQ = 2048
DEPTH = 2
HEAD_DIM = 128
ROT_DIM = HEAD_DIM // 4
ROPE_THETA = 500000.0
A_PATTERNS = ((128, 1), (512, 4), (2048, 16))
A_GROUPS = len(A_PATTERNS)
A_HEADS = 16
B_HEADS = 16
MOBA_BLOCK = 256
MOBA_TOPK = 3
N_EXPERTS = 32
TOP_K = 4
D_FF = 2048
SWIGLU_LIMIT = 7.0
SWIGLU_ALPHA = 1.702
N_A_LAYERS = DEPTH // 2
DEEPNORM_ALPHA = (2 * DEPTH) ** 0.25
LN_EPS = 1e-5
NEG_INF = -1e30

LANES = 128
V7X_VMEM_BYTES = 64 * 1024 * 1024
VMEM_LIMIT = 56 * 1024 * 1024
FFN_VMEM_LIMIT = 61 * 1024 * 1024

BF16 = jnp.bfloat16
F32 = jnp.float32


def _cparams(n_axes, vmem_limit=VMEM_LIMIT):
    return pltpu.CompilerParams(dimension_semantics=("arbitrary",) * n_axes, vmem_limit_bytes=vmem_limit)


def _rope_tables(seq):
    half = ROT_DIM // 2
    inv = ROPE_THETA ** (-jnp.arange(0, ROT_DIM, 2, dtype=F32) / ROT_DIM)
    ang = jnp.arange(seq, dtype=F32)[:, None] * inv[None, :]
    cos, sin = jnp.cos(ang), jnp.sin(ang)
    zeros = jnp.zeros((seq, HEAD_DIM - ROT_DIM), F32)
    c = jnp.concatenate([cos, cos, jnp.ones((seq, HEAD_DIM - ROT_DIM), F32)], axis=1)
    s1 = jnp.concatenate([-sin, jnp.zeros((seq, half), F32), zeros], axis=1)
    s2 = jnp.concatenate([jnp.zeros((seq, half), F32), sin, zeros], axis=1)
    return jnp.stack([c, s1, s2])


def _rope_tile(x, c, s1, s2):
    half = ROT_DIM // 2
    return x * c + pltpu.roll(x, HEAD_DIM - half, 1) * s1 + pltpu.roll(x, half, 1) * s2


def _proj_kernel(a_ref, w_ref, tab_ref, o_ref, acc_s, *, tn):
    @pl.when(pl.program_id(0) == 0)
    def _():
        acc_s[...] = jnp.zeros_like(acc_s)

    prev = acc_s[...]
    c, s1, s2 = tab_ref[0], tab_ref[1], tab_ref[2]
    for h in range(tn // HEAD_DIM):
        sl = slice(h * HEAD_DIM, (h + 1) * HEAD_DIM)
        o_ref[:, sl] = _rope_tile(prev[:, sl], c, s1, s2).astype(o_ref.dtype)
    acc_s[...] = jnp.dot(a_ref[...], w_ref[...].astype(BF16), preferred_element_type=F32)


def _proj(a, w, tabs, *, col0, n_out, pos_period, rope_mod, rope_cnt, tm=1024, tn=1024):
    m, k = a.shape
    assert m % tm == 0 and n_out % tn == 0 and col0 % tn == 0 and pos_period % tm == 0 and D_MODEL % tn == 0
    jb = col0 // tn
    per = pos_period // tm
    ident = jnp.stack([jnp.ones_like(tabs[0]), jnp.zeros_like(tabs[0]), jnp.zeros_like(tabs[0])])
    tabs2 = jnp.stack([tabs, ident])

    nj = n_out // tn
    n_tiles = (m // tm) * nj

    def plain(j):
        return jnp.where(((j * tn) // D_MODEL) % rope_mod < rope_cnt, 0, 1)

    cur = lambda t: jnp.minimum(t, n_tiles - 1)
    fin = lambda t: jnp.maximum(t - 1, 0)
    return pl.pallas_call(
        functools.partial(_proj_kernel, tn=tn),
        grid=(n_tiles + 1,),
        in_specs=[
            pl.BlockSpec((tm, k), lambda t: (cur(t) // nj, 0)),
            pl.BlockSpec((k, tn), lambda t: (0, cur(t) % nj + jb)),
            pl.BlockSpec((None, 3, tm, HEAD_DIM), lambda t: (plain(fin(t) % nj), 0, (fin(t) // nj) % per, 0)),
        ],
        out_specs=pl.BlockSpec((tm, tn), lambda t: (fin(t) // nj, fin(t) % nj)),
        out_shape=jax.ShapeDtypeStruct((m, n_out), BF16),
        scratch_shapes=[pltpu.VMEM((tm, tn), F32)],
        compiler_params=_cparams(1),
        name="proj_rope",
    )(a, w, tabs2)


def _band_attn_kernel(*refs, has_prev, n_heads, blk):
    if has_prev:
        q_ref, kp_ref, kc_ref, vp_ref, vc_ref, o_ref, lse_ref = refs
    else:
        q_ref, kc_ref, vc_ref, o_ref, lse_ref = refs
    i = pl.program_id(1)
    nk = 2 * blk if has_prev else blk
    qi = lax.broadcasted_iota(jnp.int32, (blk, nk), 0)
    ki = lax.broadcasted_iota(jnp.int32, (blk, nk), 1) - (nk - blk)
    diff = qi - ki
    mask = (diff >= 0) & (diff <= blk)
    if has_prev:
        mask = mask & ((i > 0) | (ki >= 0))
    scale = HEAD_DIM ** -0.5
    lane = lax.broadcasted_iota(jnp.int32, (blk, n_heads), 1)
    lse_all = jnp.zeros((blk, n_heads), F32)
    for h in range(n_heads):
        sl = slice(h * HEAD_DIM, (h + 1) * HEAD_DIM)
        q = q_ref[:, sl]
        if has_prev:
            k = jnp.concatenate([kp_ref[:, sl], kc_ref[:, sl]], axis=0)
            v = jnp.concatenate([vp_ref[:, sl], vc_ref[:, sl]], axis=0)
        else:
            k, v = kc_ref[:, sl], vc_ref[:, sl]
        s = lax.dot_general(q, k, (((1,), (1,)), ((), ())), preferred_element_type=F32) * scale
        s = jnp.where(mask, s, NEG_INF)
        m = jnp.max(s, axis=1, keepdims=True)
        p = jnp.exp(s - m)
        den = jnp.sum(p, axis=1, keepdims=True)
        o = jnp.dot(p.astype(BF16), v, preferred_element_type=F32) / den
        o_ref[:, sl] = o.astype(o_ref.dtype)
        lse_all = jnp.where(lane == h, m + jnp.log(den), lse_all)
    lse_ref[...] = lse_all


def _band_attn(qkv, n_heads=A_HEADS, blk=128):
    n, L, c3 = qkv.shape
    c = c3 // 3
    assert c == n_heads * HEAD_DIM and L % blk == 0
    nb = L // blk
    has_prev = True
    spec = lambda col, prev: pl.BlockSpec(
        (None, blk, c), (lambda s, i: (s, jnp.maximum(i - 1, 0), col)) if prev else (lambda s, i: (s, i, col)))
    if has_prev:
        in_specs = [spec(0, False), spec(1, True), spec(1, False), spec(2, True), spec(2, False)]
        args = (qkv,) * 5
    else:
        in_specs = [spec(0, False), spec(1, False), spec(2, False)]
        args = (qkv,) * 3
    kern = functools.partial(_band_attn_kernel, has_prev=has_prev, n_heads=n_heads, blk=blk)
    return pl.pallas_call(
        kern,
        grid=(n, nb),
        in_specs=in_specs,
        out_specs=[pl.BlockSpec((None, blk, c), lambda s, i: (s, i, 0)),
                   pl.BlockSpec((None, blk, n_heads), lambda s, i: (s, i, 0))],
        out_shape=[jax.ShapeDtypeStruct((n, L, c), BF16), jax.ShapeDtypeStruct((n, L, n_heads), F32)],
        compiler_params=_cparams(2),
        name="band_attn",
    )(*args)


def _mix_kernel(o0, o1, o2, l0, l1, l2, out_ref, *, n_heads):
    ls = [l0[...], l1[...], l2[...]]
    mx = jnp.maximum(jnp.maximum(ls[0], ls[1]), ls[2])
    es = [jnp.exp(l - mx) for l in ls]
    tot = es[0] + es[1] + es[2]
    ws = [e / tot for e in es]
    os_ = [o0, o1, o2]
    for h in range(n_heads):
        sl = slice(h * HEAD_DIM, (h + 1) * HEAD_DIM)
        acc = ws[0][:, h:h + 1] * os_[0][:, sl].astype(F32)
        acc += ws[1][:, h:h + 1] * os_[1][:, sl].astype(F32)
        acc += ws[2][:, h:h + 1] * os_[2][:, sl].astype(F32)
        out_ref[:, sl] = acc.astype(out_ref.dtype)


def _mix(os_, ls, tm=256, n_heads=A_HEADS):
    m, c = os_[0].shape
    ospec = pl.BlockSpec((tm, c), lambda i: (i, 0))
    lspec = pl.BlockSpec((tm, n_heads), lambda i: (i, 0))
    return pl.pallas_call(
        functools.partial(_mix_kernel, n_heads=n_heads),
        grid=(m // tm,),
        in_specs=[ospec] * 3 + [lspec] * 3,
        out_specs=ospec,
        out_shape=jax.ShapeDtypeStruct((m, c), BF16),
        compiler_params=_cparams(1),
        name="group_mix",
    )(*os_, *ls)


def _layer_norm_rows(y, g, b):
    mu = jnp.mean(y, axis=1, keepdims=True)
    yc = y - mu
    var = jnp.mean(yc * yc, axis=1, keepdims=True)
    return yc * lax.rsqrt(var + LN_EPS) * g + b


def _pack_pairs(y):
    half = y.shape[1] // 2
    lo = lax.bitcast_convert_type(y[:, :half].astype(BF16).astype(F32), jnp.uint32)
    hi = lax.bitcast_convert_type(y[:, half:].astype(BF16).astype(F32), jnp.uint32)
    return hi | (lo >> 16)


def _unpack_pairs(u):
    lo = lax.bitcast_convert_type(u << 16, F32).astype(BF16)
    hi = lax.bitcast_convert_type(u & jnp.uint32(0xFFFF0000), F32).astype(BF16)
    return lo, hi


def _oproj_ln_kernel(a_ref, w_ref, h_ref, g_ref, b_ref, of_ref, ob_ref, op_ref, acc_s):
    @pl.when(pl.program_id(0) == 0)
    def _():
        acc_s[...] = jnp.zeros_like(acc_s)

    y = _layer_norm_rows(DEEPNORM_ALPHA * h_ref[...] + acc_s[...], g_ref[...], b_ref[...])
    of_ref[...] = y
    ob_ref[...] = y.astype(ob_ref.dtype)
    op_ref[...] = _pack_pairs(y)
    acc_s[...] = jnp.dot(a_ref[...], w_ref[...], preferred_element_type=F32)


def _oproj_ln(a, w_bf, h, g, b, tm=512):
    m, k = a.shape
    d = w_bf.shape[1]
    row = lambda i: (i, 0)
    const = lambda i: (0, 0)
    n_tiles = m // tm
    cur = lambda t: (jnp.minimum(t, n_tiles - 1), 0)
    fin = lambda t: (jnp.maximum(t - 1, 0), 0)
    return pl.pallas_call(
        _oproj_ln_kernel,
        grid=(n_tiles + 1,),
        in_specs=[pl.BlockSpec((tm, k), cur), pl.BlockSpec((k, d), const), pl.BlockSpec((tm, d), fin),
                  pl.BlockSpec((1, d), const), pl.BlockSpec((1, d), const)],
        out_specs=[pl.BlockSpec((tm, d), fin), pl.BlockSpec((tm, d), fin), pl.BlockSpec((tm, d // 2), fin)],
        out_shape=[jax.ShapeDtypeStruct((m, d), F32), jax.ShapeDtypeStruct((m, d), BF16),
                   jax.ShapeDtypeStruct((m, d // 2), jnp.uint32)],
        scratch_shapes=[pltpu.VMEM((tm, d), F32)],
        compiler_params=_cparams(1),
        name="oproj_ln",
    )(a, w_bf, h, g.reshape(1, d), b.reshape(1, d))


def _moba_kernel(q_ref, k_ref, v_ref, et_ref, o_ref, km_s, *, heads, seq, blk):
    c = pl.program_id(2)
    nblk = seq // blk
    hw = heads * HEAD_DIM
    scale = HEAD_DIM ** -0.5

    @pl.when(c == 0)
    def _():
        ar = lax.broadcasted_iota(jnp.int32, (16, seq), 0)
        ac = lax.broadcasted_iota(jnp.int32, (16, seq), 1)
        avg = jnp.where(ac // blk == ar, 1.0 / blk, 0.0).astype(BF16)
        kmean = jnp.dot(avg, k_ref[...], preferred_element_type=F32)[0:nblk]
        kmt = jnp.concatenate([kmean] * (LANES // nblk), axis=0)
        kr = lax.broadcasted_iota(jnp.int32, (LANES, hw), 0)
        kc = lax.broadcasted_iota(jnp.int32, (LANES, hw), 1)
        km_s[...] = jnp.where(kr // nblk == kc // HEAD_DIM, kmt, 0.0).astype(BF16)

    gate = lax.dot_general(q_ref[...], km_s[...], (((1,), (1,)), ((), ())),
                           preferred_element_type=F32)
    lane = lax.broadcasted_iota(jnp.int32, (blk, LANES), 1)
    n = lane % nblk
    valid = (n < c) & (lane < heads * nblk)
    g = jnp.where(valid, gate, NEG_INF)
    cnt = jnp.zeros((blk, LANES), jnp.int32)
    for sh in range(1, nblk):
        lo = pltpu.roll(g, sh, 1)
        cnt = cnt + jnp.where((n >= sh) & (lo >= g), 1, 0)
        hi = pltpu.roll(g, LANES - sh, 1)
        cnt = cnt + jnp.where((n + sh < nblk) & (hi > g), 1, 0)
    sel = (cnt < MOBA_TOPK) & valid
    bias = jnp.where(sel | (n == c), 0.0, NEG_INF).astype(F32)

    for nb in range(2, nblk + 1, 2):
        @pl.when(2 * (c // 2 + 1) == nb)
        def _():
            kw = nb * blk
            past = kw - 2 * blk
            qpos = c * blk + lax.broadcasted_iota(jnp.int32, (blk, 2 * blk), 0)
            kpos = past + lax.broadcasted_iota(jnp.int32, (blk, 2 * blk), 1)
            causal = kpos <= qpos
            et = et_ref[0:kw, :]
            for h in range(heads):
                sl = slice(h * HEAD_DIM, (h + 1) * HEAD_DIM)
                bias_h = bias if h == 0 else pltpu.roll(bias, LANES - h * nblk, 1)
                q_aug = jnp.concatenate([q_ref[:, sl], bias_h.astype(BF16)], axis=1)
                k_aug = jnp.concatenate([k_ref[0:kw, sl], et], axis=1)
                s = lax.dot_general(q_aug, k_aug, (((1,), (1,)), ((), ())), preferred_element_type=F32) * scale
                tail = jnp.where(causal, s[:, past:], NEG_INF)
                s = tail if past == 0 else jnp.concatenate([s[:, :past], tail], axis=1)
                m = jnp.max(s, axis=1, keepdims=True)
                p = jnp.exp(s - m)
                den = jnp.sum(p, axis=1, keepdims=True)
                o = jnp.dot(p.astype(BF16), v_ref[0:kw, sl], preferred_element_type=F32) / den
                o_ref[:, sl] = o.astype(o_ref.dtype)


def _moba(q, kv, heads_per_step=8, blk=MOBA_BLOCK):
    b, s, c = q.shape
    n_heads = c // HEAD_DIM
    nblk = s // blk
    assert nblk % 2 == 0 and LANES % nblk == 0 and heads_per_step * nblk <= LANES
    hw = heads_per_step * HEAD_DIM
    ng = n_heads // heads_per_step
    et = (jnp.arange(s, dtype=jnp.int32)[:, None] // blk == jnp.arange(LANES, dtype=jnp.int32)[None, :]).astype(BF16)
    kern = functools.partial(_moba_kernel, heads=heads_per_step, seq=s, blk=blk)
    return pl.pallas_call(
        kern,
        grid=(b, ng, s // blk),
        in_specs=[pl.BlockSpec((None, blk, hw), lambda bi, g, t: (bi, t, g)),
                  pl.BlockSpec((None, s, hw), lambda bi, g, t: (bi, 0, g)),
                  pl.BlockSpec((None, s, hw), lambda bi, g, t: (bi, 0, g + ng)),
                  pl.BlockSpec((s, LANES), lambda bi, g, t: (0, 0))],
        out_specs=pl.BlockSpec((None, blk, hw), lambda bi, g, t: (bi, t, g)),
        out_shape=jax.ShapeDtypeStruct((b, s, c), BF16),
        scratch_shapes=[pltpu.VMEM((LANES, hw), BF16)],
        compiler_params=_cparams(3),
        name="moba_attn",
    )(q, kv, kv, et)


def _router_kernel(h_ref, wr_ref, br_ref, idx_ref, w_ref, rank_ref, cnt_ref, *, tm):
    t = pl.program_id(0)

    @pl.when(t == 0)
    def _():
        cnt_ref[...] = jnp.zeros_like(cnt_ref)

    logits = lax.dot_general(wr_ref[...], h_ref[...], (((1,), (1,)), ((), ())),
                             preferred_element_type=F32) + br_ref[...]
    row = lax.broadcasted_iota(jnp.int32, (N_EXPERTS, tm), 0)
    rem = logits
    vals, idxs, hots = [], [], []
    for _ in range(TOP_K):
        mx = jnp.max(rem, axis=0, keepdims=True)
        ix = jnp.min(jnp.where(rem == mx, row, N_EXPERTS), axis=0, keepdims=True)
        hot = row == ix
        vals.append(mx)
        idxs.append(ix)
        hots.append(hot)
        rem = jnp.where(hot, -jnp.inf, rem)
    es = [jnp.exp(v - vals[0]) for v in vals]
    tot = es[0] + es[1] + es[2] + es[3]
    sel = jnp.zeros((N_EXPERTS, tm), F32)
    for hot in hots:
        sel = sel + hot.astype(F32)
    ri = lax.broadcasted_iota(jnp.int32, (tm, tm), 0)
    ci = lax.broadcasted_iota(jnp.int32, (tm, tm), 1)
    upper = jnp.where(ri <= ci, 1.0, 0.0).astype(BF16)
    incl = jnp.dot(sel.astype(BF16), upper, preferred_element_type=F32)
    base = cnt_ref[:, 0:1]
    rank_e = base + incl - sel
    ranks = [jnp.sum(jnp.where(hot, rank_e, 0.0), axis=0, keepdims=True) for hot in hots]
    idx_ref[...] = jnp.concatenate(idxs, axis=0)
    w_ref[...] = jnp.concatenate([e / tot for e in es], axis=0)
    rank_ref[...] = jnp.concatenate(ranks, axis=0).astype(jnp.int32)
    cnt_ref[...] = jnp.broadcast_to(base + incl[:, tm - 1:tm], cnt_ref.shape)


def _router(h_bf, w_router, b_router, tm=512):
    n_tok, d = h_bf.shape
    wr_t = w_router.T.astype(BF16)
    br = b_router.reshape(N_EXPERTS, 1).astype(F32)
    tok = lambda i: (0, i)
    const = lambda i: (0, 0)
    return pl.pallas_call(
        functools.partial(_router_kernel, tm=tm),
        grid=(n_tok // tm,),
        in_specs=[pl.BlockSpec((tm, d), lambda i: (i, 0)), pl.BlockSpec((N_EXPERTS, d), const),
                  pl.BlockSpec((N_EXPERTS, 1), const)],
        out_specs=[pl.BlockSpec((TOP_K, tm), tok), pl.BlockSpec((TOP_K, tm), tok), pl.BlockSpec((TOP_K, tm), tok),
                   pl.BlockSpec((N_EXPERTS, LANES), const)],
        out_shape=[jax.ShapeDtypeStruct((TOP_K, n_tok), jnp.int32), jax.ShapeDtypeStruct((TOP_K, n_tok), F32),
                   jax.ShapeDtypeStruct((TOP_K, n_tok), jnp.int32), jax.ShapeDtypeStruct((N_EXPERTS, LANES), F32)],
        compiler_params=_cparams(1),
        name="moe_router",
    )(h_bf, wr_t, br)


SUB = 256
SUPER = 1024
SUBS_PER_SUPER = SUPER // SUB


ROW_UNROLL = 16
ROW_GRAN = 128


def _moe_tables(counts, n_slots):
    n_super_max = N_EXPERTS + n_slots // SUPER
    n_sub_e = (counts + SUB - 1) // SUB
    xs_start = SUB * (jnp.cumsum(n_sub_e) - n_sub_e)
    n_sb_e = (n_sub_e + SUBS_PER_SUPER - 1) // SUBS_PER_SUPER
    per_e = jnp.maximum((n_sub_e + jnp.maximum(n_sb_e, 1) - 1) // jnp.maximum(n_sb_e, 1), 1)
    sb_end = jnp.cumsum(n_sb_e)
    sb_start = sb_end - n_sb_e
    total = sb_end[-1]
    g = jnp.arange(n_super_max, dtype=jnp.int32)
    gc = jnp.minimum(g, total - 1)
    e_of = jnp.minimum(jnp.searchsorted(sb_end, gc, side="right"), N_EXPERTS - 1).astype(jnp.int32)
    j_in = gc - sb_start[e_of]
    blk0 = xs_start[e_of] // SUB + per_e[e_of] * j_in
    nsub = jnp.clip(n_sub_e[e_of] - per_e[e_of] * j_in, 0, per_e[e_of])
    rows = jnp.clip(counts[e_of] - SUB * per_e[e_of] * j_in, 0, SUB * nsub)
    nsub_active = jnp.where(g < total, (rows + ROW_GRAN - 1) // ROW_GRAN, 0)
    ys_start = SUPER * sb_start
    xs_tail = xs_start + SUB * jnp.maximum(n_sub_e - 1, 0)
    i32 = lambda a: a.astype(jnp.int32)
    return dict(xs_start=i32(xs_start), ys_start=i32(ys_start), per=i32(per_e), xs_tail=i32(xs_tail),
                e_of=i32(e_of), blk0=i32(blk0), nsub=i32(nsub), nsub_active=i32(nsub_active), g_out=i32(gc))


def _slot_rows_kernel(xs_start_ref, ys_start_ref, per_ref, idx_ref, rank_ref, xr_ref, yr_ref):
    idx = idx_ref[...]
    rank = rank_ref[...]
    xs0 = jnp.zeros_like(idx)
    ys0 = jnp.zeros_like(idx)
    per = jnp.ones_like(idx)
    for e in range(N_EXPERTS):
        hit = idx == e
        xs0 = jnp.where(hit, xs_start_ref[e], xs0)
        ys0 = jnp.where(hit, ys_start_ref[e], ys0)
        per = jnp.where(hit, per_ref[e], per)
    sub = jnp.right_shift(rank, SUB.bit_length() - 1)
    q = jnp.floor((sub.astype(F32) + 0.5) / per.astype(F32)).astype(jnp.int32)
    xr_ref[...] = xs0 + rank
    yr_ref[...] = ys0 + q * SUPER + (rank - q * per * SUB)


def _slot_rows(top_idx, rank, tables, tm=2048):
    k, n_tok = top_idx.shape
    tm = min(tm, n_tok)
    blk = pl.BlockSpec((k, tm), lambda i, *_: (0, i))
    grid_spec = pltpu.PrefetchScalarGridSpec(
        num_scalar_prefetch=3, grid=(n_tok // tm,), in_specs=[blk, blk], out_specs=[blk, blk])
    return pl.pallas_call(
        _slot_rows_kernel,
        grid_spec=grid_spec,
        out_shape=[jax.ShapeDtypeStruct((k, n_tok), jnp.int32)] * 2,
        compiler_params=_cparams(1),
        name="moe_slot_rows",
    )(tables["xs_start"], tables["ys_start"], tables["per"], top_idx, rank)


def _dispatch_kernel(tail_ref, used_ref, row_ref, h_ref, xs_hbm, zbuf, hbuf, sems, zsem, *, tm, n_tiles):
    i = pl.program_id(0)
    slot = i % 2
    n_groups = tm // ROW_UNROLL

    def tail_copy(e):
        return pltpu.make_async_copy(zbuf, xs_hbm.at[pl.ds(pl.multiple_of(tail_ref[e], SUB), SUB)], zsem)

    @pl.when(i == 0)
    def _():
        zbuf[...] = jnp.zeros_like(zbuf)
        for e in range(N_EXPERTS):
            @pl.when(used_ref[e] > 0)
            def _():
                tail_copy(e).start()
        for e in range(N_EXPERTS):
            @pl.when(used_ref[e] > 0)
            def _():
                tail_copy(e).wait()

    def drain(s):
        def body(gi, carry):
            t0 = pl.multiple_of(gi * ROW_UNROLL, ROW_UNROLL)
            for u in range(ROW_UNROLL):
                for k in range(TOP_K):
                    pltpu.make_async_copy(hbuf.at[s, pl.ds(t0 + u, 1)], xs_hbm.at[pl.ds(0, 1)], sems.at[s]).wait()
            return carry
        lax.fori_loop(0, n_groups, body, 0)

    @pl.when(i >= 2)
    def _():
        drain(slot)

    hbuf[slot] = h_ref[...]

    def issue(gi, carry):
        t0 = pl.multiple_of(gi * ROW_UNROLL, ROW_UNROLL)
        for u in range(ROW_UNROLL):
            for k in range(TOP_K):
                pltpu.make_async_copy(hbuf.at[slot, pl.ds(t0 + u, 1)], xs_hbm.at[pl.ds(row_ref[k, t0 + u], 1)],
                                      sems.at[slot]).start()
        return carry

    lax.fori_loop(0, n_groups, issue, 0)

    @pl.when(i == n_tiles - 1)
    def _():
        if n_tiles >= 2:
            drain(1 - slot)
        drain(slot)


def _dispatch(hp, xs_row, tables, counts, n_rows, tm=256):
    n_tok, dp = hp.shape
    n_tiles = n_tok // tm
    grid_spec = pltpu.PrefetchScalarGridSpec(
        num_scalar_prefetch=2,
        grid=(n_tiles,),
        in_specs=[pl.BlockSpec((TOP_K, tm), lambda i, *_: (0, i), memory_space=pltpu.SMEM),
                  pl.BlockSpec((tm, dp), lambda i, *_: (i, 0))],
        out_specs=pl.BlockSpec(memory_space=pl.ANY),
        scratch_shapes=[pltpu.VMEM((SUB, dp), hp.dtype), pltpu.VMEM((2, tm, dp), hp.dtype),
                        pltpu.SemaphoreType.DMA((2,)), pltpu.SemaphoreType.DMA],
    )
    return pl.pallas_call(
        functools.partial(_dispatch_kernel, tm=tm, n_tiles=n_tiles),
        grid_spec=grid_spec,
        out_shape=jax.ShapeDtypeStruct((n_rows, dp), hp.dtype),
        compiler_params=_cparams(1),
        name="moe_dispatch",
    )(tables["xs_tail"], counts, xs_row, hp)


def _ffn_kernel(e_ref, blk0_ref, nsub_ref, act_ref, out_ref_idx, x0, x1, x2, x3, wg_ref, wu_ref, bg_ref, bu_ref,
                wd_ref, bd_ref, y_ref, xb, *, tf):
    del e_ref, blk0_ref, nsub_ref, out_ref_idx
    g = pl.program_id(0)
    j = pl.program_id(1)
    nact = act_ref[g]

    @pl.when(j == 0)
    def _():
        half = xb.shape[1] // 2
        for s, x_ref in enumerate((x0, x1, x2, x3)):
            @pl.when(s * (SUB // ROW_GRAN) < nact)
            def _():
                lo, hi = _unpack_pairs(x_ref[...])
                xb[s * SUB:(s + 1) * SUB, :half] = lo
                xb[s * SUB:(s + 1) * SUB, half:] = hi

    for n in range(1, SUPER // ROW_GRAN + 1):
        @pl.when(nact == n)
        def _():
            m = n * ROW_GRAN
            x = xb[0:m, :]
            gate = jnp.dot(x, wg_ref[...].astype(BF16), preferred_element_type=F32) + bg_ref[...]
            lin = jnp.dot(x, wu_ref[...].astype(BF16), preferred_element_type=F32) + bu_ref[...]
            gate = jnp.minimum(gate, SWIGLU_LIMIT)
            lin = jnp.clip(lin, -SWIGLU_LIMIT, SWIGLU_LIMIT)
            hid = (lin + 1.0) * gate * jax.nn.sigmoid(SWIGLU_ALPHA * gate)

            @pl.when(j == 0)
            def _():
                y_ref[0:m, :] = jnp.broadcast_to(bd_ref[...], (m, y_ref.shape[1]))

            y_ref[0:m, :] += jnp.dot(hid.astype(BF16), wd_ref[...].astype(BF16), preferred_element_type=F32)


def _expert_ffn(xs, tables, layer, w_gu, b_gu, w_down, b_down, n_super_max, tf=512):
    e_of, blk0, nsub, nsub_active, g_out = (tables[k] for k in ("e_of", "blk0", "nsub", "nsub_active", "g_out"))
    n_rows, dp = xs.shape
    d = 2 * dp
    n_we = w_gu.shape[0] * N_EXPERTS
    e_of = e_of + layer * N_EXPERTS
    w_gu = w_gu.reshape(n_we, d, 2 * D_FF)
    w_down = w_down.reshape(n_we, D_FF, d)
    b_gu = b_gu.reshape(n_we, 1, 2 * D_FF)
    b_down = b_down.reshape(n_we, 1, d)
    n_ff = D_FF // tf
    last = n_ff - 1

    def x_spec(s):
        return pl.BlockSpec(
            (SUB, dp), lambda g, j, e, b0, ns, na, go: (b0[g] + jnp.minimum(s, jnp.maximum(ns[g], 1) - 1), 0))

    def jj(j, na, g):
        return jnp.where(na[g] > 0, j, last)

    in_specs = [x_spec(s) for s in range(SUBS_PER_SUPER)] + [
        pl.BlockSpec((None, d, tf), lambda g, j, e, b0, ns, na, go: (e[g], 0, jj(j, na, g))),
        pl.BlockSpec((None, d, tf), lambda g, j, e, b0, ns, na, go: (e[g], 0, n_ff + jj(j, na, g))),
        pl.BlockSpec((None, 1, tf), lambda g, j, e, b0, ns, na, go: (e[g], 0, jj(j, na, g))),
        pl.BlockSpec((None, 1, tf), lambda g, j, e, b0, ns, na, go: (e[g], 0, n_ff + jj(j, na, g))),
        pl.BlockSpec((None, tf, d), lambda g, j, e, b0, ns, na, go: (e[g], jj(j, na, g), 0)),
        pl.BlockSpec((None, 1, d), lambda g, j, e, b0, ns, na, go: (e[g], 0, 0)),
    ]
    grid_spec = pltpu.PrefetchScalarGridSpec(
        num_scalar_prefetch=5,
        grid=(n_super_max, n_ff),
        in_specs=in_specs,
        out_specs=pl.BlockSpec((SUPER, d), lambda g, j, e, b0, ns, na, go: (go[g], 0)),
        scratch_shapes=[pltpu.VMEM((SUPER, d), BF16)],
    )
    return pl.pallas_call(
        functools.partial(_ffn_kernel, tf=tf),
        grid_spec=grid_spec,
        out_shape=jax.ShapeDtypeStruct((n_super_max * SUPER, d), F32),
        compiler_params=_cparams(2, FFN_VMEM_LIMIT),
        name="moe_ffn",
    )(e_of, blk0, nsub, nsub_active, g_out, xs, xs, xs, xs, w_gu, w_gu, b_gu, b_gu, w_down, b_down)


def _combine_kernel(row_ref, next_ref, w_ref, h_ref, g_ref, b_ref, ys_hbm, of_ref, ob_ref, op_ref, ybuf, sems,
                    *, tm, n_tiles):
    i = pl.program_id(0)
    slot = i % 2
    n_groups = tm // ROW_UNROLL

    def issue_group(rows_smem, s, gi):
        t0 = pl.multiple_of(gi * ROW_UNROLL, ROW_UNROLL)
        for u in range(ROW_UNROLL):
            for k in range(TOP_K):
                pltpu.make_async_copy(ys_hbm.at[pl.ds(rows_smem[k, t0 + u], 1)],
                                      ybuf.at[s, k, pl.ds(t0 + u, 1)], sems.at[s]).start()

    def wait_group(gi, carry):
        t0 = pl.multiple_of(gi * ROW_UNROLL, ROW_UNROLL)
        for u in range(ROW_UNROLL):
            for k in range(TOP_K):
                pltpu.make_async_copy(ys_hbm.at[pl.ds(0, 1)], ybuf.at[slot, k, pl.ds(t0 + u, 1)],
                                      sems.at[slot]).wait()
        return carry

    @pl.when(i == 0)
    def _():
        def first(gi, carry):
            issue_group(row_ref, 0, gi)
            return carry
        lax.fori_loop(0, n_groups, first, 0)

    lax.fori_loop(0, n_groups, wait_group, 0)

    @pl.when(i + 1 < n_tiles)
    def _():
        def ahead(gi, carry):
            issue_group(next_ref, 1 - slot, gi)
            return carry
        lax.fori_loop(0, n_groups, ahead, 0)

    w = w_ref[...]
    ffn = w[:, 0:1] * ybuf[slot, 0]
    for k in range(1, TOP_K):
        ffn = ffn + w[:, k:k + 1] * ybuf[slot, k]
    y = _layer_norm_rows(DEEPNORM_ALPHA * h_ref[...] + ffn, g_ref[...], b_ref[...])
    of_ref[...] = y
    ob_ref[...] = y.astype(ob_ref.dtype)
    op_ref[...] = _pack_pairs(y)


def _combine_ln(ys, ys_row, w_t, h, g, b, tm=128):
    n_tok, d = h.shape
    n_tiles = n_tok // tm
    row = lambda i: (i, 0)
    const = lambda i: (0, 0)
    return pl.pallas_call(
        functools.partial(_combine_kernel, tm=tm, n_tiles=n_tiles),
        grid=(n_tiles,),
        in_specs=[pl.BlockSpec((TOP_K, tm), lambda i: (0, i), memory_space=pltpu.SMEM),
                  pl.BlockSpec((TOP_K, tm), lambda i: (0, jnp.minimum(i + 1, n_tiles - 1)), memory_space=pltpu.SMEM),
                  pl.BlockSpec((tm, TOP_K), row), pl.BlockSpec((tm, d), row),
                  pl.BlockSpec((1, d), const), pl.BlockSpec((1, d), const), pl.BlockSpec(memory_space=pl.ANY)],
        out_specs=[pl.BlockSpec((tm, d), row), pl.BlockSpec((tm, d), row), pl.BlockSpec((tm, d // 2), row)],
        out_shape=[jax.ShapeDtypeStruct((n_tok, d), F32), jax.ShapeDtypeStruct((n_tok, d), BF16),
                   jax.ShapeDtypeStruct((n_tok, d // 2), jnp.uint32)],
        scratch_shapes=[pltpu.VMEM((2, TOP_K, tm, d), F32), pltpu.SemaphoreType.DMA((2,))],
        compiler_params=_cparams(1),
        name="moe_combine_ln",
    )(ys_row, ys_row, w_t, h, g.reshape(1, d), b.reshape(1, d), ys)


def _moe_ln(h, h_bf, h_pk, layer, w_router, b_router, w_gu, b_gu, w_down, b_down, ln_g, ln_b):
    n_tok, d = h.shape
    n_slots = n_tok * TOP_K
    top_idx, top_w, rank, cnt = _router(h_bf, w_router, b_router)
    counts = cnt[:, 0].astype(jnp.int32)
    tables = _moe_tables(counts, n_slots)
    xs_row, ys_row = _slot_rows(top_idx, rank, tables)
    n_super_max = N_EXPERTS + n_slots // SUPER
    n_rows = n_slots + N_EXPERTS * SUB
    xs = _dispatch(h_pk, xs_row, tables, counts, n_rows)
    ys = _expert_ffn(xs, tables, layer, w_gu, b_gu, w_down, b_down, n_super_max)
    return _combine_ln(ys, ys_row, top_w.T, h, ln_g, ln_b)


def _dilated_mix(x3, w_qkv, tabs):
    b, s, d = x3.shape
    n_tok = b * s
    width = A_HEADS * HEAD_DIM
    outs, lses = [], []
    for g, (window, dil) in enumerate(A_PATTERNS):
        assert window // dil == 128
        L = s // dil
        xp = x3.reshape(b, L, dil, d).transpose(0, 2, 1, 3).reshape(n_tok, d).astype(BF16)
        tabs_p = tabs.reshape(3, L, dil, HEAD_DIM).transpose(0, 2, 1, 3).reshape(3, s, HEAD_DIM)
        qkv = _proj(xp, w_qkv, tabs_p, col0=g * 3 * width, n_out=3 * width, pos_period=s, rope_mod=3, rope_cnt=2)
        o, lse = _band_attn(qkv.reshape(b * dil, L, 3 * width))
        outs.append(o.reshape(b, dil, L, width).transpose(0, 2, 1, 3).reshape(n_tok, width))
        lses.append(lse.reshape(b, dil, L, A_HEADS).transpose(0, 2, 1, 3).reshape(n_tok, A_HEADS))
    return _mix(outs, lses)


def kernel(x, a_w_qkv, a_w_o, kv_w, b_w_q, b_w_o, router_w, router_b, moe_w_gate_up, moe_b_gate_up, moe_w_down,
           moe_b_down, ln1_g, ln1_b, ln2_g, ln2_b):
    b, s, d = x.shape
    n_tok = b * s
    tabs = _rope_tables(s)
    h = x.reshape(n_tok, d)
    h_bf = h.astype(BF16)
    kv = None
    for layer in range(DEPTH):
        if layer < N_A_LAYERS:
            mix = _dilated_mix(h.reshape(b, s, d), a_w_qkv[layer], tabs)
            w_o = a_w_o[layer]
        else:
            j = layer - N_A_LAYERS
            if layer == N_A_LAYERS:
                kv = _proj(h_bf, kv_w, tabs, col0=0, n_out=2 * B_HEADS * HEAD_DIM, pos_period=s, rope_mod=2,
                           rope_cnt=1)
            q = _proj(h_bf, b_w_q[j], tabs, col0=0, n_out=B_HEADS * HEAD_DIM, pos_period=s, rope_mod=1, rope_cnt=1)
            mix = _moba(q.reshape(b, s, -1), kv.reshape(b, s, -1)).reshape(n_tok, -1)
            w_o = b_w_o[j]
        h, h_bf, h_pk = _oproj_ln(mix, w_o.astype(BF16), h, ln1_g[layer], ln1_b[layer])
        h, h_bf, _ = _moe_ln(h, h_bf, h_pk, layer, router_w[layer], router_b[layer], moe_w_gate_up, moe_b_gate_up,
                             moe_w_down, moe_b_down, ln2_g[layer], ln2_b[layer])
    return h.reshape(b, s, d)
```

```python
import functools

import jax
import jax.numpy as jnp
from jax import lax
from jax.experimental import pallas as pl
from jax.experimental.pallas import tpu as pltpu

D_MODEL = 2048
SEQ = 2048
DEPTH = 2
HEAD_DIM = 128
ROT_DIM = HEAD_DIM // 4
ROPE_THETA = 500000.0
A_PATTERNS = ((128, 1), (512, 4), (2048, 16))
A_GROUPS = len(A_PATTERNS)
A_HEADS = 16
B_HEADS = 16
MOBA_BLOCK = 256
MOBA_TOPK = 3
N_EXPERTS = 32
TOP_K = 4
D_FF = 2048
SWIGLU_LIMIT = 7.0
SWIGLU_ALPHA = 1.702
N_A_LAYERS = DEPTH // 2
DEEPNORM_ALPHA = (2 * DEPTH) ** 0.25
LN_EPS = 1e-5
NEG_INF = -1e30

LANES = 128
V7X_VMEM_BYTES = 64 * 1024 * 1024
VMEM_LIMIT = 56 * 1024 * 1024

BF16 = jnp.bfloat16
F32 = jnp.float32


def _cparams(n_axes, vmem_limit=VMEM_LIMIT):
    return pltpu.CompilerParams(dimension_semantics=("arbitrary",) * n_axes, vmem_limit_bytes=vmem_limit)


def _rope_tables(seq):
    half = ROT_DIM // 2
    inv = ROPE_THETA ** (-jnp.arange(0, ROT_DIM, 2, dtype=F32) / ROT_DIM)
    ang = jnp.arange(seq, dtype=F32)[:, None] * inv[None, :]
    cos, sin = jnp.cos(ang), jnp.sin(ang)
    zeros = jnp.zeros((seq, HEAD_DIM - ROT_DIM), F32)
    c = jnp.concatenate([cos, cos, jnp.ones((seq, HEAD_DIM - ROT_DIM), F32)], axis=1)
    s1 = jnp.concatenate([-sin, jnp.zeros((seq, half), F32), zeros], axis=1)
    s2 = jnp.concatenate([jnp.zeros((seq, half), F32), sin, zeros], axis=1)
    return jnp.stack([c, s1, s2])


def _rope_tile(x, c, s1, s2):
    half = ROT_DIM // 2
    return x * c + pltpu.roll(x, HEAD_DIM - half, 1) * s1 + pltpu.roll(x, half, 1) * s2


def _proj_kernel(a_ref, w_ref, tab_ref, o_ref, acc_s, *, tn):
    @pl.when(pl.program_id(0) == 0)
    def _():
        acc_s[...] = jnp.zeros_like(acc_s)

    prev = acc_s[...]
    c, s1, s2 = tab_ref[0], tab_ref[1], tab_ref[2]
    for h in range(tn // HEAD_DIM):
        sl = slice(h * HEAD_DIM, (h + 1) * HEAD_DIM)
        o_ref[:, sl] = _rope_tile(prev[:, sl], c, s1, s2).astype(o_ref.dtype)
    acc_s[...] = jnp.dot(a_ref[...], w_ref[...].astype(BF16), preferred_element_type=F32)


def _proj(a, w, tabs, *, col0, n_out, pos_period, rope_mod, rope_cnt, tm=1024, tn=1024):
    m, k = a.shape
    assert m % tm == 0 and n_out % tn == 0 and col0 % tn == 0 and pos_period % tm == 0 and D_MODEL % tn == 0
    jb = col0 // tn
    per = pos_period // tm
    ident = jnp.stack([jnp.ones_like(tabs[0]), jnp.zeros_like(tabs[0]), jnp.zeros_like(tabs[0])])
    tabs2 = jnp.stack([tabs, ident])

    nj = n_out // tn
    n_tiles = (m // tm) * nj

    def plain(j):
        return jnp.where(((j * tn) // D_MODEL) % rope_mod < rope_cnt, 0, 1)

    cur = lambda t: jnp.minimum(t, n_tiles - 1)
    fin = lambda t: jnp.maximum(t - 1, 0)
    return pl.pallas_call(
        functools.partial(_proj_kernel, tn=tn),
        grid=(n_tiles + 1,),
        in_specs=[
            pl.BlockSpec((tm, k), lambda t: (cur(t) // nj, 0)),
            pl.BlockSpec((k, tn), lambda t: (0, cur(t) % nj + jb)),
            pl.BlockSpec((None, 3, tm, HEAD_DIM), lambda t: (plain(fin(t) % nj), 0, (fin(t) // nj) % per, 0)),
        ],
        out_specs=pl.BlockSpec((tm, tn), lambda t: (fin(t) // nj, fin(t) % nj)),
        out_shape=jax.ShapeDtypeStruct((m, n_out), BF16),
        scratch_shapes=[pltpu.VMEM((tm, tn), F32)],
        compiler_params=_cparams(1),
        name="proj_rope",
    )(a, w, tabs2)


def _band_attn_kernel(*refs, has_prev, n_heads, blk):
    if has_prev:
        q_ref, kp_ref, kc_ref, vp_ref, vc_ref, o_ref, lse_ref = refs
    else:
        q_ref, kc_ref, vc_ref, o_ref, lse_ref = refs
    i = pl.program_id(1)
    nk = 2 * blk if has_prev else blk
    qi = lax.broadcasted_iota(jnp.int32, (blk, nk), 0)
    ki = lax.broadcasted_iota(jnp.int32, (blk, nk), 1) - (nk - blk)
    diff = qi - ki
    mask = (diff >= 0) & (diff <= blk)
    if has_prev:
        mask = mask & ((i > 0) | (ki >= 0))
    scale = HEAD_DIM ** -0.5
    lane = lax.broadcasted_iota(jnp.int32, (blk, n_heads), 1)
    lse_all = jnp.zeros((blk, n_heads), F32)
    for h in range(n_heads):
        sl = slice(h * HEAD_DIM, (h + 1) * HEAD_DIM)
        q = q_ref[:, sl]
        if has_prev:
            k = jnp.concatenate([kp_ref[:, sl], kc_ref[:, sl]], axis=0)
            v = jnp.concatenate([vp_ref[:, sl], vc_ref[:, sl]], axis=0)
        else:
            k, v = kc_ref[:, sl], vc_ref[:, sl]
        s = lax.dot_general(q, k, (((1,), (1,)), ((), ())), preferred_element_type=F32) * scale
        s = jnp.where(mask, s, NEG_INF)
        m = jnp.max(s, axis=1, keepdims=True)
        p = jnp.exp(s - m)
        den = jnp.sum(p, axis=1, keepdims=True)
        o = jnp.dot(p.astype(BF16), v, preferred_element_type=F32) / den
        o_ref[:, sl] = o.astype(o_ref.dtype)
        lse_all = jnp.where(lane == h, m + jnp.log(den), lse_all)
    lse_ref[...] = lse_all


def _band_attn(qkv, n_heads=A_HEADS, blk=128):
    n, L, c3 = qkv.shape
    c = c3 // 3
    assert c == n_heads * HEAD_DIM and L % blk == 0
    nb = L // blk
    has_prev = True
    spec = lambda col, prev: pl.BlockSpec(
        (None, blk, c), (lambda s, i: (s, jnp.maximum(i - 1, 0), col)) if prev else (lambda s, i: (s, i, col)))
    if has_prev:
        in_specs = [spec(0, False), spec(1, True), spec(1, False), spec(2, True), spec(2, False)]
        args = (qkv,) * 5
    else:
        in_specs = [spec(0, False), spec(1, False), spec(2, False)]
        args = (qkv,) * 3
    kern = functools.partial(_band_attn_kernel, has_prev=has_prev, n_heads=n_heads, blk=blk)
    return pl.pallas_call(
        kern,
        grid=(n, nb),
        in_specs=in_specs,
        out_specs=[pl.BlockSpec((None, blk, c), lambda s, i: (s, i, 0)),
                   pl.BlockSpec((None, blk, n_heads), lambda s, i: (s, i, 0))],
        out_shape=[jax.ShapeDtypeStruct((n, L, c), BF16), jax.ShapeDtypeStruct((n, L, n_heads), F32)],
        compiler_params=_cparams(2),
        name="band_attn",
    )(*args)


def _mix_kernel(o0, o1, o2, l0, l1, l2, out_ref, *, n_heads):
    ls = [l0[...], l1[...], l2[...]]
    mx = jnp.maximum(jnp.maximum(ls[0], ls[1]), ls[2])
    es = [jnp.exp(l - mx) for l in ls]
    tot = es[0] + es[1] + es[2]
    ws = [e / tot for e in es]
    os_ = [o0, o1, o2]
    for h in range(n_heads):
        sl = slice(h * HEAD_DIM, (h + 1) * HEAD_DIM)
        acc = ws[0][:, h:h + 1] * os_[0][:, sl].astype(F32)
        acc += ws[1][:, h:h + 1] * os_[1][:, sl].astype(F32)
        acc += ws[2][:, h:h + 1] * os_[2][:, sl].astype(F32)
        out_ref[:, sl] = acc.astype(out_ref.dtype)


def _mix(os_, ls, tm=256, n_heads=A_HEADS):
    m, c = os_[0].shape
    ospec = pl.BlockSpec((tm, c), lambda i: (i, 0))
    lspec = pl.BlockSpec((tm, n_heads), lambda i: (i, 0))
    return pl.pallas_call(
        functools.partial(_mix_kernel, n_heads=n_heads),
        grid=(m // tm,),
        in_specs=[ospec] * 3 + [lspec] * 3,
        out_specs=ospec,
        out_shape=jax.ShapeDtypeStruct((m, c), BF16),
        compiler_params=_cparams(1),
        name="group_mix",
    )(*os_, *ls)


def _layer_norm_rows(y, g, b):
    mu = jnp.mean(y, axis=1, keepdims=True)
    yc = y - mu
    var = jnp.mean(yc * yc, axis=1, keepdims=True)
    return yc * lax.rsqrt(var + LN_EPS) * g + b


def _pack_pairs(y):
    half = y.shape[1] // 2
    lo = lax.bitcast_convert_type(y[:, :half].astype(BF16).astype(F32), jnp.uint32)
    hi = lax.bitcast_convert_type(y[:, half:].astype(BF16).astype(F32), jnp.uint32)
    return hi | (lo >> 16)


def _unpack_pairs(u):
    lo = lax.bitcast_convert_type(u << 16, F32).astype(BF16)
    hi = lax.bitcast_convert_type(u & jnp.uint32(0xFFFF0000), F32).astype(BF16)
    return lo, hi


def _oproj_ln_kernel(a_ref, w_ref, h_ref, g_ref, b_ref, of_ref, ob_ref, op_ref, acc_s):
    @pl.when(pl.program_id(0) == 0)
    def _():
        acc_s[...] = jnp.zeros_like(acc_s)

    y = _layer_norm_rows(DEEPNORM_ALPHA * h_ref[...] + acc_s[...], g_ref[...], b_ref[...])
    of_ref[...] = y
    ob_ref[...] = y.astype(ob_ref.dtype)
    op_ref[...] = _pack_pairs(y)
    acc_s[...] = jnp.dot(a_ref[...], w_ref[...], preferred_element_type=F32)


def _oproj_ln(a, w_bf, h, g, b, tm=512):
    m, k = a.shape
    d = w_bf.shape[1]
    row = lambda i: (i, 0)
    const = lambda i: (0, 0)
    n_tiles = m // tm
    cur = lambda t: (jnp.minimum(t, n_tiles - 1), 0)
    fin = lambda t: (jnp.maximum(t - 1, 0), 0)
    return pl.pallas_call(
        _oproj_ln_kernel,
        grid=(n_tiles + 1,),
        in_specs=[pl.BlockSpec((tm, k), cur), pl.BlockSpec((k, d), const), pl.BlockSpec((tm, d), fin),
                  pl.BlockSpec((1, d), const), pl.BlockSpec((1, d), const)],
        out_specs=[pl.BlockSpec((tm, d), fin), pl.BlockSpec((tm, d), fin), pl.BlockSpec((tm, d // 2), fin)],
        out_shape=[jax.ShapeDtypeStruct((m, d), F32), jax.ShapeDtypeStruct((m, d), BF16),
                   jax.ShapeDtypeStruct((m, d // 2), jnp.uint32)],
        scratch_shapes=[pltpu.VMEM((tm, d), F32)],
        compiler_params=_cparams(1),
        name="oproj_ln",
    )(a, w_bf, h, g.reshape(1, d), b.reshape(1, d))


def _moba_kernel(q_ref, k_ref, v_ref, et_ref, o_ref, km_s, *, heads, seq, blk):
    c = pl.program_id(2)
    nblk = seq // blk
    hw = heads * HEAD_DIM
    scale = HEAD_DIM ** -0.5

    @pl.when(c == 0)
    def _():
        ar = lax.broadcasted_iota(jnp.int32, (16, seq), 0)
        ac = lax.broadcasted_iota(jnp.int32, (16, seq), 1)
        avg = jnp.where(ac // blk == ar, 1.0 / blk, 0.0).astype(BF16)
        kmean = jnp.dot(avg, k_ref[...], preferred_element_type=F32)[0:nblk]
        kmt = jnp.concatenate([kmean] * (LANES // nblk), axis=0)
        kr = lax.broadcasted_iota(jnp.int32, (LANES, hw), 0)
        kc = lax.broadcasted_iota(jnp.int32, (LANES, hw), 1)
        km_s[...] = jnp.where(kr // nblk == kc // HEAD_DIM, kmt, 0.0).astype(BF16)

    gate = lax.dot_general(q_ref[...], km_s[...], (((1,), (1,)), ((), ())),
                           preferred_element_type=F32)
    lane = lax.broadcasted_iota(jnp.int32, (blk, LANES), 1)
    n = lane % nblk
    valid = (n < c) & (lane < heads * nblk)
    g = jnp.where(valid, gate, NEG_INF)
    cnt = jnp.zeros((blk, LANES), jnp.int32)
    for sh in range(1, nblk):
        lo = pltpu.roll(g, sh, 1)
        cnt = cnt + jnp.where((n >= sh) & (lo >= g), 1, 0)
        hi = pltpu.roll(g, LANES - sh, 1)
        cnt = cnt + jnp.where((n + sh < nblk) & (hi > g), 1, 0)
    sel = (cnt < MOBA_TOPK) & valid
    bias = jnp.where(sel | (n == c), 0.0, NEG_INF).astype(F32)

    for nb in range(2, nblk + 1, 2):
        @pl.when(2 * (c // 2 + 1) == nb)
        def _():
            kw = nb * blk
            past = kw - 2 * blk
            qpos = c * blk + lax.broadcasted_iota(jnp.int32, (blk, 2 * blk), 0)
            kpos = past + lax.broadcasted_iota(jnp.int32, (blk, 2 * blk), 1)
            causal = kpos <= qpos
            et = et_ref[0:kw, :]
            for h in range(heads):
                sl = slice(h * HEAD_DIM, (h + 1) * HEAD_DIM)
                bias_h = bias if h == 0 else pltpu.roll(bias, LANES - h * nblk, 1)
                q_aug = jnp.concatenate([q_ref[:, sl], bias_h.astype(BF16)], axis=1)
                k_aug = jnp.concatenate([k_ref[0:kw, sl], et], axis=1)
                s = lax.dot_general(q_aug, k_aug, (((1,), (1,)), ((), ())), preferred_element_type=F32) * scale
                tail = jnp.where(causal, s[:, past:], NEG_INF)
                s = tail if past == 0 else jnp.concatenate([s[:, :past], tail], axis=1)
                m = jnp.max(s, axis=1, keepdims=True)
                p = jnp.exp(s - m)
                den = jnp.sum(p, axis=1, keepdims=True)
                o = jnp.dot(p.astype(BF16), v_ref[0:kw, sl], preferred_element_type=F32) / den
                o_ref[:, sl] = o.astype(o_ref.dtype)


def _moba(q, kv, heads_per_step=8, blk=MOBA_BLOCK):
    b, s, c = q.shape
    n_heads = c // HEAD_DIM
    nblk = s // blk
    assert nblk % 2 == 0 and LANES % nblk == 0 and heads_per_step * nblk <= LANES
    hw = heads_per_step * HEAD_DIM
    ng = n_heads // heads_per_step
    et = (jnp.arange(s, dtype=jnp.int32)[:, None] // blk == jnp.arange(LANES, dtype=jnp.int32)[None, :]).astype(BF16)
    kern = functools.partial(_moba_kernel, heads=heads_per_step, seq=s, blk=blk)
    return pl.pallas_call(
        kern,
        grid=(b, ng, s // blk),
        in_specs=[pl.BlockSpec((None, blk, hw), lambda bi, g, t: (bi, t, g)),
                  pl.BlockSpec((None, s, hw), lambda bi, g, t: (bi, 0, g)),
                  pl.BlockSpec((None, s, hw), lambda bi, g, t: (bi, 0, g + ng)),
                  pl.BlockSpec((s, LANES), lambda bi, g, t: (0, 0))],
        out_specs=pl.BlockSpec((None, blk, hw), lambda bi, g, t: (bi, t, g)),
        out_shape=jax.ShapeDtypeStruct((b, s, c), BF16),
        scratch_shapes=[pltpu.VMEM((LANES, hw), BF16)],
        compiler_params=_cparams(3),
        name="moba_attn",
    )(q, kv, kv, et)


def _router_kernel(h_ref, wr_ref, br_ref, idx_ref, w_ref, rank_ref, cnt_ref, *, tm):
    t = pl.program_id(0)

    @pl.when(t == 0)
    def _():
        cnt_ref[...] = jnp.zeros_like(cnt_ref)

    logits = lax.dot_general(wr_ref[...], h_ref[...], (((1,), (1,)), ((), ())),
                             preferred_element_type=F32) + br_ref[...]
    row = lax.broadcasted_iota(jnp.int32, (N_EXPERTS, tm), 0)
    rem = logits
    vals, idxs, hots = [], [], []
    for _ in range(TOP_K):
        mx = jnp.max(rem, axis=0, keepdims=True)
        ix = jnp.min(jnp.where(rem == mx, row, N_EXPERTS), axis=0, keepdims=True)
        hot = row == ix
        vals.append(mx)
        idxs.append(ix)
        hots.append(hot)
        rem = jnp.where(hot, -jnp.inf, rem)
    es = [jnp.exp(v - vals[0]) for v in vals]
    tot = es[0] + es[1] + es[2] + es[3]
    sel = jnp.zeros((N_EXPERTS, tm), F32)
    for hot in hots:
        sel = sel + hot.astype(F32)
    ri = lax.broadcasted_iota(jnp.int32, (tm, tm), 0)
    ci = lax.broadcasted_iota(jnp.int32, (tm, tm), 1)
    upper = jnp.where(ri <= ci, 1.0, 0.0).astype(BF16)
    incl = jnp.dot(sel.astype(BF16), upper, preferred_element_type=F32)
    base = cnt_ref[:, 0:1]
    rank_e = base + incl - sel
    ranks = [jnp.sum(jnp.where(hot, rank_e, 0.0), axis=0, keepdims=True) for hot in hots]
    idx_ref[...] = jnp.concatenate(idxs, axis=0)
    w_ref[...] = jnp.concatenate([e / tot for e in es], axis=0)
    rank_ref[...] = jnp.concatenate(ranks, axis=0).astype(jnp.int32)
    cnt_ref[...] = jnp.broadcast_to(base + incl[:, tm - 1:tm], cnt_ref.shape)


def _router(h_bf, w_router, b_router, tm=512):
    n_tok, d = h_bf.shape
    wr_t = w_router.T.astype(BF16)
    br = b_router.reshape(N_EXPERTS, 1).astype(F32)
    tok = lambda i: (0, i)
    const = lambda i: (0, 0)
    return pl.pallas_call(
        functools.partial(_router_kernel, tm=tm),
        grid=(n_tok // tm,),
        in_specs=[pl.BlockSpec((tm, d), lambda i: (i, 0)), pl.BlockSpec((N_EXPERTS, d), const),
                  pl.BlockSpec((N_EXPERTS, 1), const)],
        out_specs=[pl.BlockSpec((TOP_K, tm), tok), pl.BlockSpec((TOP_K, tm), tok), pl.BlockSpec((TOP_K, tm), tok),
                   pl.BlockSpec((N_EXPERTS, LANES), const)],
        out_shape=[jax.ShapeDtypeStruct((TOP_K, n_tok), jnp.int32), jax.ShapeDtypeStruct((TOP_K, n_tok), F32),
                   jax.ShapeDtypeStruct((TOP_K, n_tok), jnp.int32), jax.ShapeDtypeStruct((N_EXPERTS, LANES), F32)],
        compiler_params=_cparams(1),
        name="moe_router",
    )(h_bf, wr_t, br)


SUB = 256
SUPER = 1280
SUBS_PER_SUPER = SUPER // SUB


ROW_UNROLL = 16
ROW_GRAN = 128


def _moe_tables(counts, n_slots):
    n_super_max = N_EXPERTS + n_slots // SUPER
    n_sub_e = (counts + SUB - 1) // SUB
    xs_start = SUB * (jnp.cumsum(n_sub_e) - n_sub_e)
    n_sb_e = (n_sub_e + SUBS_PER_SUPER - 1) // SUBS_PER_SUPER
    per_e = jnp.maximum((n_sub_e + jnp.maximum(n_sb_e, 1) - 1) // jnp.maximum(n_sb_e, 1), 1)
    sb_end = jnp.cumsum(n_sb_e)
    sb_start = sb_end - n_sb_e
    total = sb_end[-1]
    g = jnp.arange(n_super_max, dtype=jnp.int32)
    gc = jnp.minimum(g, total - 1)
    e_of = jnp.minimum(jnp.searchsorted(sb_end, gc, side="right"), N_EXPERTS - 1).astype(jnp.int32)
    j_in = gc - sb_start[e_of]
    blk0 = xs_start[e_of] // SUB + per_e[e_of] * j_in
    nsub = jnp.clip(n_sub_e[e_of] - per_e[e_of] * j_in, 0, per_e[e_of])
    rows = jnp.clip(counts[e_of] - SUB * per_e[e_of] * j_in, 0, SUB * nsub)
    nsub_active = jnp.where(g < total, (rows + ROW_GRAN - 1) // ROW_GRAN, 0)
    ys_start = SUPER * sb_start
    xs_tail = xs_start + SUB * jnp.maximum(n_sub_e - 1, 0)
    i32 = lambda a: a.astype(jnp.int32)
    return dict(xs_start=i32(xs_start), ys_start=i32(ys_start), per=i32(per_e), xs_tail=i32(xs_tail),
                e_of=i32(e_of), blk0=i32(blk0), nsub=i32(nsub), nsub_active=i32(nsub_active), g_out=i32(gc))


def _slot_rows_kernel(xs_start_ref, ys_start_ref, per_ref, idx_ref, rank_ref, xr_ref, yr_ref):
    idx = idx_ref[...]
    rank = rank_ref[...]
    xs0 = jnp.zeros_like(idx)
    ys0 = jnp.zeros_like(idx)
    per = jnp.ones_like(idx)
    for e in range(N_EXPERTS):
        hit = idx == e
        xs0 = jnp.where(hit, xs_start_ref[e], xs0)
        ys0 = jnp.where(hit, ys_start_ref[e], ys0)
        per = jnp.where(hit, per_ref[e], per)
    sub = jnp.right_shift(rank, SUB.bit_length() - 1)
    q = jnp.floor((sub.astype(F32) + 0.5) / per.astype(F32)).astype(jnp.int32)
    xr_ref[...] = xs0 + rank
    yr_ref[...] = ys0 + q * SUPER + (rank - q * per * SUB)


def _slot_rows(top_idx, rank, tables, tm=2048):
    k, n_tok = top_idx.shape
    tm = min(tm, n_tok)
    blk = pl.BlockSpec((k, tm), lambda i, *_: (0, i))
    grid_spec = pltpu.PrefetchScalarGridSpec(
        num_scalar_prefetch=3, grid=(n_tok // tm,), in_specs=[blk, blk], out_specs=[blk, blk])
    return pl.pallas_call(
        _slot_rows_kernel,
        grid_spec=grid_spec,
        out_shape=[jax.ShapeDtypeStruct((k, n_tok), jnp.int32)] * 2,
        compiler_params=_cparams(1),
        name="moe_slot_rows",
    )(tables["xs_start"], tables["ys_start"], tables["per"], top_idx, rank)


def _dispatch_kernel(tail_ref, used_ref, row_ref, h_ref, xs_hbm, zbuf, hbuf, sems, zsem, *, tm, n_tiles):
    i = pl.program_id(0)
    slot = i % 2
    n_groups = tm // ROW_UNROLL

    def tail_copy(e):
        return pltpu.make_async_copy(zbuf, xs_hbm.at[pl.ds(pl.multiple_of(tail_ref[e], SUB), SUB)], zsem)

    @pl.when(i == 0)
    def _():
        zbuf[...] = jnp.zeros_like(zbuf)
        for e in range(N_EXPERTS):
            @pl.when(used_ref[e] > 0)
            def _():
                tail_copy(e).start()
        for e in range(N_EXPERTS):
            @pl.when(used_ref[e] > 0)
            def _():
                tail_copy(e).wait()

    def drain(s):
        def body(gi, carry):
            t0 = pl.multiple_of(gi * ROW_UNROLL, ROW_UNROLL)
            for u in range(ROW_UNROLL):
                for k in range(TOP_K):
                    pltpu.make_async_copy(hbuf.at[s, pl.ds(t0 + u, 1)], xs_hbm.at[pl.ds(0, 1)], sems.at[s]).wait()
            return carry
        lax.fori_loop(0, n_groups, body, 0)

    @pl.when(i >= 2)
    def _():
        drain(slot)

    hbuf[slot] = h_ref[...]

    def issue(gi, carry):
        t0 = pl.multiple_of(gi * ROW_UNROLL, ROW_UNROLL)
        for u in range(ROW_UNROLL):
            for k in range(TOP_K):
                pltpu.make_async_copy(hbuf.at[slot, pl.ds(t0 + u, 1)], xs_hbm.at[pl.ds(row_ref[k, t0 + u], 1)],
                                      sems.at[slot]).start()
        return carry

    lax.fori_loop(0, n_groups, issue, 0)

    @pl.when(i == n_tiles - 1)
    def _():
        if n_tiles >= 2:
            drain(1 - slot)
        drain(slot)


def _dispatch(hp, xs_row, tables, counts, n_rows, tm=256):
    n_tok, dp = hp.shape
    n_tiles = n_tok // tm
    grid_spec = pltpu.PrefetchScalarGridSpec(
        num_scalar_prefetch=2,
        grid=(n_tiles,),
        in_specs=[pl.BlockSpec((TOP_K, tm), lambda i, *_: (0, i), memory_space=pltpu.SMEM),
                  pl.BlockSpec((tm, dp), lambda i, *_: (i, 0))],
        out_specs=pl.BlockSpec(memory_space=pl.ANY),
        scratch_shapes=[pltpu.VMEM((SUB, dp), hp.dtype), pltpu.VMEM((2, tm, dp), hp.dtype),
                        pltpu.SemaphoreType.DMA((2,)), pltpu.SemaphoreType.DMA],
    )
    return pl.pallas_call(
        functools.partial(_dispatch_kernel, tm=tm, n_tiles=n_tiles),
        grid_spec=grid_spec,
        out_shape=jax.ShapeDtypeStruct((n_rows, dp), hp.dtype),
        compiler_params=_cparams(1),
        name="moe_dispatch",
    )(tables["xs_tail"], counts, xs_row, hp)


def _ffn_kernel(e_ref, blk0_ref, nsub_ref, act_ref, out_ref_idx, *refs, tf):
    del e_ref, blk0_ref, nsub_ref, out_ref_idx
    x_refs = refs[:SUBS_PER_SUPER]
    wg_ref, wu_ref, bg_ref, bu_ref, wd_ref, bd_ref, y_ref, xb = refs[SUBS_PER_SUPER:]
    g = pl.program_id(0)
    j = pl.program_id(1)
    nact = act_ref[g]

    @pl.when(j == 0)
    def _():
        half = xb.shape[1] // 2
        for s, x_ref in enumerate(x_refs):
            @pl.when(s * (SUB // ROW_GRAN) < nact)
            def _():
                lo, hi = _unpack_pairs(x_ref[...])
                xb[s * SUB:(s + 1) * SUB, :half] = lo
                xb[s * SUB:(s + 1) * SUB, half:] = hi

    for n in range(1, SUPER // ROW_GRAN + 1):
        @pl.when(nact == n)
        def _():
            m = n * ROW_GRAN
            x = xb[0:m, :]
            gate = jnp.dot(x, wg_ref[...].astype(BF16), preferred_element_type=F32) + bg_ref[...]
            lin = jnp.dot(x, wu_ref[...].astype(BF16), preferred_element_type=F32) + bu_ref[...]
            gate = jnp.minimum(gate, SWIGLU_LIMIT)
            lin = jnp.clip(lin, -SWIGLU_LIMIT, SWIGLU_LIMIT)
            hid = (lin + 1.0) * gate * jax.nn.sigmoid(SWIGLU_ALPHA * gate)

            @pl.when(j == 0)
            def _():
                y_ref[0:m, :] = jnp.broadcast_to(bd_ref[...], (m, y_ref.shape[1]))

            y_ref[0:m, :] += jnp.dot(hid.astype(BF16), wd_ref[...].astype(BF16), preferred_element_type=F32)


def _expert_ffn(xs, tables, layer, w_gu, b_gu, w_down, b_down, n_super_max, tf=256):
    e_of, blk0, nsub, nsub_active, g_out = (tables[k] for k in ("e_of", "blk0", "nsub", "nsub_active", "g_out"))
    n_rows, dp = xs.shape
    d = 2 * dp
    n_we = w_gu.shape[0] * N_EXPERTS
    e_of = e_of + layer * N_EXPERTS
    w_gu = w_gu.reshape(n_we, d, 2 * D_FF)
    w_down = w_down.reshape(n_we, D_FF, d)
    b_gu = b_gu.reshape(n_we, 1, 2 * D_FF)
    b_down = b_down.reshape(n_we, 1, d)
    n_ff = D_FF // tf
    last = n_ff - 1

    def x_spec(s):
        return pl.BlockSpec(
            (SUB, dp), lambda g, j, e, b0, ns, na, go: (b0[g] + jnp.minimum(s, jnp.maximum(ns[g], 1) - 1), 0))

    def jj(j, na, g):
        return jnp.where(na[g] > 0, j, last)

    in_specs = [x_spec(s) for s in range(SUBS_PER_SUPER)] + [
        pl.BlockSpec((None, d, tf), lambda g, j, e, b0, ns, na, go: (e[g], 0, jj(j, na, g))),
        pl.BlockSpec((None, d, tf), lambda g, j, e, b0, ns, na, go: (e[g], 0, n_ff + jj(j, na, g))),
        pl.BlockSpec((None, 1, tf), lambda g, j, e, b0, ns, na, go: (e[g], 0, jj(j, na, g))),
        pl.BlockSpec((None, 1, tf), lambda g, j, e, b0, ns, na, go: (e[g], 0, n_ff + jj(j, na, g))),
        pl.BlockSpec((None, tf, d), lambda g, j, e, b0, ns, na, go: (e[g], jj(j, na, g), 0)),
        pl.BlockSpec((None, 1, d), lambda g, j, e, b0, ns, na, go: (e[g], 0, 0)),
    ]
    grid_spec = pltpu.PrefetchScalarGridSpec(
        num_scalar_prefetch=5,
        grid=(n_super_max, n_ff),
        in_specs=in_specs,
        out_specs=pl.BlockSpec((SUPER, d), lambda g, j, e, b0, ns, na, go: (go[g], 0)),
        scratch_shapes=[pltpu.VMEM((SUPER, d), BF16)],
    )
    return pl.pallas_call(
        functools.partial(_ffn_kernel, tf=tf),
        grid_spec=grid_spec,
        out_shape=jax.ShapeDtypeStruct((n_super_max * SUPER, d), F32),
        compiler_params=_cparams(2),
        name="moe_ffn",
    )(e_of, blk0, nsub, nsub_active, g_out, *([xs] * SUBS_PER_SUPER), w_gu, w_gu, b_gu, b_gu, w_down, b_down)


def _combine_kernel(row_ref, next_ref, w_ref, h_ref, g_ref, b_ref, ys_hbm, of_ref, ob_ref, op_ref, ybuf, sems,
                    *, tm, n_tiles):
    i = pl.program_id(0)
    slot = i % 2
    n_groups = tm // ROW_UNROLL

    def issue_group(rows_smem, s, gi):
        t0 = pl.multiple_of(gi * ROW_UNROLL, ROW_UNROLL)
        for u in range(ROW_UNROLL):
            for k in range(TOP_K):
                pltpu.make_async_copy(ys_hbm.at[pl.ds(rows_smem[k, t0 + u], 1)],
                                      ybuf.at[s, k, pl.ds(t0 + u, 1)], sems.at[s]).start()

    def wait_group(gi, carry):
        t0 = pl.multiple_of(gi * ROW_UNROLL, ROW_UNROLL)
        for u in range(ROW_UNROLL):
            for k in range(TOP_K):
                pltpu.make_async_copy(ys_hbm.at[pl.ds(0, 1)], ybuf.at[slot, k, pl.ds(t0 + u, 1)],
                                      sems.at[slot]).wait()
        return carry

    @pl.when(i == 0)
    def _():
        def first(gi, carry):
            issue_group(row_ref, 0, gi)
            return carry
        lax.fori_loop(0, n_groups, first, 0)

    lax.fori_loop(0, n_groups, wait_group, 0)

    @pl.when(i + 1 < n_tiles)
    def _():
        def ahead(gi, carry):
            issue_group(next_ref, 1 - slot, gi)
            return carry
        lax.fori_loop(0, n_groups, ahead, 0)

    w = w_ref[...]
    ffn = w[:, 0:1] * ybuf[slot, 0]
    for k in range(1, TOP_K):
        ffn = ffn + w[:, k:k + 1] * ybuf[slot, k]
    y = _layer_norm_rows(DEEPNORM_ALPHA * h_ref[...] + ffn, g_ref[...], b_ref[...])
    of_ref[...] = y
    ob_ref[...] = y.astype(ob_ref.dtype)
    op_ref[...] = _pack_pairs(y)


def _combine_ln(ys, ys_row, w_t, h, g, b, tm=128):
    n_tok, d = h.shape
    n_tiles = n_tok // tm
    row = lambda i: (i, 0)
    const = lambda i: (0, 0)
    return pl.pallas_call(
        functools.partial(_combine_kernel, tm=tm, n_tiles=n_tiles),
        grid=(n_tiles,),
        in_specs=[pl.BlockSpec((TOP_K, tm), lambda i: (0, i), memory_space=pltpu.SMEM),
                  pl.BlockSpec((TOP_K, tm), lambda i: (0, jnp.minimum(i + 1, n_tiles - 1)), memory_space=pltpu.SMEM),
                  pl.BlockSpec((tm, TOP_K), row), pl.BlockSpec((tm, d), row),
                  pl.BlockSpec((1, d), const), pl.BlockSpec((1, d), const), pl.BlockSpec(memory_space=pl.ANY)],
        out_specs=[pl.BlockSpec((tm, d), row), pl.BlockSpec((tm, d), row), pl.BlockSpec((tm, d // 2), row)],
        out_shape=[jax.ShapeDtypeStruct((n_tok, d), F32), jax.ShapeDtypeStruct((n_tok, d), BF16),
                   jax.ShapeDtypeStruct((n_tok, d // 2), jnp.uint32)],
        scratch_shapes=[pltpu.VMEM((2, TOP_K, tm, d), F32), pltpu.SemaphoreType.DMA((2,))],
        compiler_params=_cparams(1),
        name="moe_combine_ln",
    )(ys_row, ys_row, w_t, h, g.reshape(1, d), b.reshape(1, d), ys)


def _moe_ln(h, h_bf, h_pk, layer, w_router, b_router, w_gu, b_gu, w_down, b_down, ln_g, ln_b):
    n_tok, d = h.shape
    n_slots = n_tok * TOP_K
    top_idx, top_w, rank, cnt = _router(h_bf, w_router, b_router)
    counts = cnt[:, 0].astype(jnp.int32)
    tables = _moe_tables(counts, n_slots)
    xs_row, ys_row = _slot_rows(top_idx, rank, tables)
    n_super_max = N_EXPERTS + n_slots // SUPER
    n_rows = n_slots + N_EXPERTS * SUB
    xs = _dispatch(h_pk, xs_row, tables, counts, n_rows)
    ys = _expert_ffn(xs, tables, layer, w_gu, b_gu, w_down, b_down, n_super_max)
    return _combine_ln(ys, ys_row, top_w.T, h, ln_g, ln_b)


def _dilated_mix(x3, w_qkv, tabs):
    b, s, d = x3.shape
    n_tok = b * s
    width = A_HEADS * HEAD_DIM
    outs, lses = [], []
    for g, (window, dil) in enumerate(A_PATTERNS):
        assert window // dil == 128
        L = s // dil
        xp = x3.reshape(b, L, dil, d).transpose(0, 2, 1, 3).reshape(n_tok, d).astype(BF16)
        tabs_p = tabs.reshape(3, L, dil, HEAD_DIM).transpose(0, 2, 1, 3).reshape(3, s, HEAD_DIM)
        qkv = _proj(xp, w_qkv, tabs_p, col0=g * 3 * width, n_out=3 * width, pos_period=s, rope_mod=3, rope_cnt=2)
        o, lse = _band_attn(qkv.reshape(b * dil, L, 3 * width))
        outs.append(o.reshape(b, dil, L, width).transpose(0, 2, 1, 3).reshape(n_tok, width))
        lses.append(lse.reshape(b, dil, L, A_HEADS).transpose(0, 2, 1, 3).reshape(n_tok, A_HEADS))
    return _mix(outs, lses)


def kernel(x, a_w_qkv, a_w_o, kv_w, b_w_q, b_w_o, router_w, router_b, moe_w_gate_up, moe_b_gate_up, moe_w_down,
           moe_b_down, ln1_g, ln1_b, ln2_g, ln2_b):
    b, s, d = x.shape
    n_tok = b * s
    tabs = _rope_tables(s)
    h = x.reshape(n_tok, d)
    h_bf = h.astype(BF16)
    kv = None
    for layer in range(DEPTH):
        if layer < N_A_LAYERS:
            mix = _dilated_mix(h.reshape(b, s, d), a_w_qkv[layer], tabs)
            w_o = a_w_o[layer]
        else:
            j = layer - N_A_LAYERS
            if layer == N_A_LAYERS:
                kv = _proj(h_bf, kv_w, tabs, col0=0, n_out=2 * B_HEADS * HEAD_DIM, pos_period=s, rope_mod=2,
                           rope_cnt=1)
            q = _proj(h_bf, b_w_q[j], tabs, col0=0, n_out=B_HEADS * HEAD_DIM, pos_period=s, rope_mod=1, rope_cnt=1)
            mix = _moba(q.reshape(b, s, -1), kv.reshape(b, s, -1)).reshape(n_tok, -1)
            w_o = b_w_o[j]
        h, h_bf, h_pk = _oproj_ln(mix, w_o.astype(BF16), h, ln1_g[layer], ln1_b[layer])
        h, h_bf, _ = _moe_ln(h, h_bf, h_pk, layer, router_w[layer], router_b[layer], moe_w_gate_up, moe_b_gate_up,
                             moe_w_down, moe_b_down, ln2_g[layer], ln2_b[layer])
    return h.reshape(b, s, d)
```

```python
import functools

import jax
import jax.numpy as jnp
from jax import lax
from jax.experimental import pallas as pl
from jax.experimental.pallas import tpu as pltpu

D_MODEL = 2048
SEQ = 2048
DEPTH = 2
HEAD_DIM = 128
ROT_DIM = HEAD_DIM // 4
ROPE_THETA = 500000.0
A_PATTERNS = ((128, 1), (512, 4), (2048, 16))
A_GROUPS = len(A_PATTERNS)
A_HEADS = 16
B_HEADS = 16
MOBA_BLOCK = 256
MOBA_TOPK = 3
N_EXPERTS = 32
TOP_K = 4
D_FF = 2048
SWIGLU_LIMIT = 7.0
SWIGLU_ALPHA = 1.702
N_A_LAYERS = DEPTH // 2
DEEPNORM_ALPHA = (2 * DEPTH) ** 0.25
LN_EPS = 1e-5
NEG_INF = -1e30
LOG2_E = 1.4426950408889634
LN_2 = 0.6931471805599453

LANES = 128
V7X_VMEM_BYTES = 64 * 1024 * 1024
VMEM_LIMIT = 56 * 1024 * 1024
FFN_VMEM_LIMIT = 61 * 1024 * 1024

BF16 = jnp.bfloat16
F32 = jnp.float32


def _cparams(n_axes, vmem_limit=VMEM_LIMIT):
    return pltpu.CompilerParams(dimension_semantics=("arbitrary",) * n_axes, vmem_limit_bytes=vmem_limit)


def _rope_tables(seq):
    half = ROT_DIM // 2
    inv = ROPE_THETA ** (-jnp.arange(0, ROT_DIM, 2, dtype=F32) / ROT_DIM)
    ang = jnp.arange(seq, dtype=F32)[:, None] * inv[None, :]
    cos, sin = jnp.cos(ang), jnp.sin(ang)
    zeros = jnp.zeros((seq, HEAD_DIM - ROT_DIM), F32)
    c = jnp.concatenate([cos, cos, jnp.ones((seq, HEAD_DIM - ROT_DIM), F32)], axis=1)
    s1 = jnp.concatenate([-sin, jnp.zeros((seq, half), F32), zeros], axis=1)
    s2 = jnp.concatenate([jnp.zeros((seq, half), F32), sin, zeros], axis=1)
    return jnp.stack([c, s1, s2])


def _rope_tile(x, c, s1, s2):
    half = ROT_DIM // 2
    return x * c + pltpu.roll(x, HEAD_DIM - half, 1) * s1 + pltpu.roll(x, half, 1) * s2


def _proj_kernel(a_ref, w_ref, tab_ref, o_ref, acc_s, *abf, tn, nj, n_tiles):
    t = pl.program_id(0)

    @pl.when(t == 0)
    def _():
        acc_s[...] = jnp.zeros_like(acc_s)

    if abf:
        @pl.when(jnp.minimum(t, n_tiles - 1) % nj == 0)
        def _():
            abf[0][...] = a_ref[...].astype(BF16)
        a_src = abf[0]
    else:
        a_src = a_ref

    prev = acc_s[...]
    c, s1, s2 = tab_ref[0], tab_ref[1], tab_ref[2]
    for h in range(tn // HEAD_DIM):
        sl = slice(h * HEAD_DIM, (h + 1) * HEAD_DIM)
        o_ref[:, sl] = _rope_tile(prev[:, sl], c, s1, s2).astype(o_ref.dtype)
    acc_s[...] = jnp.dot(a_src[...], w_ref[...].astype(BF16), preferred_element_type=F32)


def _proj(a, w, tabs, *, col0, n_out, pos_period, rope_mod, rope_cnt, tm=1024, tn=1024):
    m, k = a.shape
    assert m % tm == 0 and n_out % tn == 0 and col0 % tn == 0 and pos_period % tm == 0 and D_MODEL % tn == 0
    jb = col0 // tn
    per = pos_period // tm
    ident = jnp.stack([jnp.ones_like(tabs[0]), jnp.zeros_like(tabs[0]), jnp.zeros_like(tabs[0])])
    tabs2 = jnp.stack([tabs, ident])

    nj = n_out // tn
    n_tiles = (m // tm) * nj

    def plain(j):
        return jnp.where(((j * tn) // D_MODEL) % rope_mod < rope_cnt, 0, 1)

    cur = lambda t: jnp.minimum(t, n_tiles - 1)
    fin = lambda t: jnp.maximum(t - 1, 0)
    scratch = [pltpu.VMEM((tm, tn), F32)]
    if a.dtype != BF16:
        assert nj > 1
        scratch.append(pltpu.VMEM((tm, k), BF16))
    return pl.pallas_call(
        functools.partial(_proj_kernel, tn=tn, nj=nj, n_tiles=n_tiles),
        grid=(n_tiles + 1,),
        in_specs=[
            pl.BlockSpec((tm, k), lambda t: (cur(t) // nj, 0)),
            pl.BlockSpec((k, tn), lambda t: (0, cur(t) % nj + jb)),
            pl.BlockSpec((None, 3, tm, HEAD_DIM), lambda t: (plain(fin(t) % nj), 0, (fin(t) // nj) % per, 0)),
        ],
        out_specs=pl.BlockSpec((tm, tn), lambda t: (fin(t) // nj, fin(t) % nj)),
        out_shape=jax.ShapeDtypeStruct((m, n_out), BF16),
        scratch_shapes=scratch,
        compiler_params=_cparams(1),
        name="proj_rope",
    )(a, w, tabs2)


def _band_attn_kernel(*refs, has_prev, n_heads, blk):
    if has_prev:
        q_ref, kp_ref, kc_ref, vp_ref, vc_ref, o_ref, lse_ref = refs
    else:
        q_ref, kc_ref, vc_ref, o_ref, lse_ref = refs
    i = pl.program_id(1)
    nk = 2 * blk if has_prev else blk
    qi = lax.broadcasted_iota(jnp.int32, (blk, nk), 0)
    ki = lax.broadcasted_iota(jnp.int32, (blk, nk), 1) - (nk - blk)
    diff = qi - ki
    mask = (diff >= 0) & (diff <= blk)
    if has_prev:
        mask = mask & ((i > 0) | (ki >= 0))
    scale = HEAD_DIM ** -0.5
    lane = lax.broadcasted_iota(jnp.int32, (blk, n_heads), 1)
    lse_all = jnp.zeros((blk, n_heads), F32)
    for h in range(n_heads):
        sl = slice(h * HEAD_DIM, (h + 1) * HEAD_DIM)
        q = q_ref[:, sl]
        if has_prev:
            k = jnp.concatenate([kp_ref[:, sl], kc_ref[:, sl]], axis=0)
            v = jnp.concatenate([vp_ref[:, sl], vc_ref[:, sl]], axis=0)
        else:
            k, v = kc_ref[:, sl], vc_ref[:, sl]
        s = lax.dot_general(q, k, (((1,), (1,)), ((), ())), preferred_element_type=F32) * (scale * LOG2_E)
        s = jnp.where(mask, s, NEG_INF)
        m = jnp.max(s, axis=1, keepdims=True)
        p = jnp.exp2(s - m)
        den = jnp.sum(p, axis=1, keepdims=True)
        o = jnp.dot(p.astype(BF16), v, preferred_element_type=F32) / den
        o_ref[:, sl] = o.astype(o_ref.dtype)
        lse_all = jnp.where(lane == h, m * LN_2 + jnp.log(den), lse_all)
    lse_ref[...] = lse_all


def _band_attn(qkv, n_heads=A_HEADS, blk=128):
    n, L, c3 = qkv.shape
    c = c3 // 3
    assert c == n_heads * HEAD_DIM and L % blk == 0
    nb = L // blk
    has_prev = True
    spec = lambda col, prev: pl.BlockSpec(
        (None, blk, c), (lambda s, i: (s, jnp.maximum(i - 1, 0), col)) if prev else (lambda s, i: (s, i, col)))
    if has_prev:
        in_specs = [spec(0, False), spec(1, True), spec(1, False), spec(2, True), spec(2, False)]
        args = (qkv,) * 5
    else:
        in_specs = [spec(0, False), spec(1, False), spec(2, False)]
        args = (qkv,) * 3
    kern = functools.partial(_band_attn_kernel, has_prev=has_prev, n_heads=n_heads, blk=blk)
    return pl.pallas_call(
        kern,
        grid=(n, nb),
        in_specs=in_specs,
        out_specs=[pl.BlockSpec((None, blk, c), lambda s, i: (s, i, 0)),
                   pl.BlockSpec((None, blk, n_heads), lambda s, i: (s, i, 0))],
        out_shape=[jax.ShapeDtypeStruct((n, L, c), BF16), jax.ShapeDtypeStruct((n, L, n_heads), F32)],
        compiler_params=_cparams(2),
        name="band_attn",
    )(*args)


def _mix_kernel(o0, o1, o2, l0, l1, l2, out_ref, *, n_heads):
    ls = [l0[...], l1[...], l2[...]]
    mx = jnp.maximum(jnp.maximum(ls[0], ls[1]), ls[2])
    es = [jnp.exp(l - mx) for l in ls]
    tot = es[0] + es[1] + es[2]
    ws = [e / tot for e in es]
    os_ = [o0, o1, o2]
    for h in range(n_heads):
        sl = slice(h * HEAD_DIM, (h + 1) * HEAD_DIM)
        acc = ws[0][:, h:h + 1] * os_[0][:, sl].astype(F32)
        acc += ws[1][:, h:h + 1] * os_[1][:, sl].astype(F32)
        acc += ws[2][:, h:h + 1] * os_[2][:, sl].astype(F32)
        out_ref[:, sl] = acc.astype(out_ref.dtype)


def _mix(os_, ls, tm=256, n_heads=A_HEADS):
    m, c = os_[0].shape
    ospec = pl.BlockSpec((tm, c), lambda i: (i, 0))
    lspec = pl.BlockSpec((tm, n_heads), lambda i: (i, 0))
    return pl.pallas_call(
        functools.partial(_mix_kernel, n_heads=n_heads),
        grid=(m // tm,),
        in_specs=[ospec] * 3 + [lspec] * 3,
        out_specs=ospec,
        out_shape=jax.ShapeDtypeStruct((m, c), BF16),
        compiler_params=_cparams(1),
        name="group_mix",
    )(*os_, *ls)


def _layer_norm_rows(y, g, b):
    mu = jnp.mean(y, axis=1, keepdims=True)
    yc = y - mu
    var = jnp.mean(yc * yc, axis=1, keepdims=True)
    return yc * lax.rsqrt(var + LN_EPS) * g + b


def _pack_pairs(y):
    half = y.shape[1] // 2
    lo = lax.bitcast_convert_type(y[:, :half].astype(BF16).astype(F32), jnp.uint32)
    hi = lax.bitcast_convert_type(y[:, half:].astype(BF16).astype(F32), jnp.uint32)
    return hi | (lo >> 16)


def _unpack_pairs(u):
    lo = lax.bitcast_convert_type(u << 16, F32).astype(BF16)
    hi = lax.bitcast_convert_type(u & jnp.uint32(0xFFFF0000), F32).astype(BF16)
    return lo, hi


def _oproj_ln_kernel(a_ref, w_ref, h_ref, g_ref, b_ref, of_ref, ob_ref, op_ref, acc_s):
    @pl.when(pl.program_id(0) == 0)
    def _():
        acc_s[...] = jnp.zeros_like(acc_s)

    y = _layer_norm_rows(DEEPNORM_ALPHA * h_ref[...] + acc_s[...], g_ref[...], b_ref[...])
    of_ref[...] = y
    ob_ref[...] = y.astype(ob_ref.dtype)
    op_ref[...] = _pack_pairs(y)
    acc_s[...] = jnp.dot(a_ref[...], w_ref[...], preferred_element_type=F32)


def _oproj_ln(a, w_bf, h, g, b, tm=512):
    m, k = a.shape
    d = w_bf.shape[1]
    row = lambda i: (i, 0)
    const = lambda i: (0, 0)
    n_tiles = m // tm
    cur = lambda t: (jnp.minimum(t, n_tiles - 1), 0)
    fin = lambda t: (jnp.maximum(t - 1, 0), 0)
    return pl.pallas_call(
        _oproj_ln_kernel,
        grid=(n_tiles + 1,),
        in_specs=[pl.BlockSpec((tm, k), cur), pl.BlockSpec((k, d), const), pl.BlockSpec((tm, d), fin),
                  pl.BlockSpec((1, d), const), pl.BlockSpec((1, d), const)],
        out_specs=[pl.BlockSpec((tm, d), fin), pl.BlockSpec((tm, d), fin), pl.BlockSpec((tm, d // 2), fin)],
        out_shape=[jax.ShapeDtypeStruct((m, d), F32), jax.ShapeDtypeStruct((m, d), BF16),
                   jax.ShapeDtypeStruct((m, d // 2), jnp.uint32)],
        scratch_shapes=[pltpu.VMEM((tm, d), F32)],
        compiler_params=_cparams(1),
        name="oproj_ln",
    )(a, w_bf, h, g.reshape(1, d), b.reshape(1, d))


def _moba_kernel(q_ref, k_ref, v_ref, et_ref, o_ref, km_s, *, heads, seq, blk):
    c = pl.program_id(2)
    nblk = seq // blk
    hw = heads * HEAD_DIM
    scale = HEAD_DIM ** -0.5

    @pl.when(c == 0)
    def _():
        ar = lax.broadcasted_iota(jnp.int32, (16, seq), 0)
        ac = lax.broadcasted_iota(jnp.int32, (16, seq), 1)
        avg = jnp.where(ac // blk == ar, 1.0 / blk, 0.0).astype(BF16)
        kmean = jnp.dot(avg, k_ref[...], preferred_element_type=F32)[0:nblk]
        kmt = jnp.concatenate([kmean] * (LANES // nblk), axis=0)
        kr = lax.broadcasted_iota(jnp.int32, (LANES, hw), 0)
        kc = lax.broadcasted_iota(jnp.int32, (LANES, hw), 1)
        km_s[...] = jnp.where(kr // nblk == kc // HEAD_DIM, kmt, 0.0).astype(BF16)

    gate = lax.dot_general(q_ref[...], km_s[...], (((1,), (1,)), ((), ())),
                           preferred_element_type=F32)
    lane = lax.broadcasted_iota(jnp.int32, (blk, LANES), 1)
    n = lane % nblk
    valid = (n < c) & (lane < heads * nblk)
    g = jnp.where(valid, gate, NEG_INF)
    cnt = jnp.zeros((blk, LANES), jnp.int32)
    for sh in range(1, nblk):
        lo = pltpu.roll(g, sh, 1)
        cnt = cnt + jnp.where((n >= sh) & (lo >= g), 1, 0)
        hi = pltpu.roll(g, LANES - sh, 1)
        cnt = cnt + jnp.where((n + sh < nblk) & (hi > g), 1, 0)
    sel = (cnt < MOBA_TOPK) & valid
    bias = jnp.where(sel | (n == c), 0.0, NEG_INF).astype(F32)

    for nb in range(2, nblk + 1, 2):
        @pl.when(2 * (c // 2 + 1) == nb)
        def _():
            kw = nb * blk
            past = kw - 2 * blk
            qpos = c * blk + lax.broadcasted_iota(jnp.int32, (blk, 2 * blk), 0)
            kpos = past + lax.broadcasted_iota(jnp.int32, (blk, 2 * blk), 1)
            causal = kpos <= qpos
            et = et_ref[0:kw, :]
            for h in range(heads):
                sl = slice(h * HEAD_DIM, (h + 1) * HEAD_DIM)
                bias_h = bias if h == 0 else pltpu.roll(bias, LANES - h * nblk, 1)
                q_aug = jnp.concatenate([q_ref[:, sl], bias_h.astype(BF16)], axis=1)
                k_aug = jnp.concatenate([k_ref[0:kw, sl], et], axis=1)
                s = lax.dot_general(q_aug, k_aug, (((1,), (1,)), ((), ())),
                                    preferred_element_type=F32) * (scale * LOG2_E)
                tail = jnp.where(causal, s[:, past:], NEG_INF)
                s = tail if past == 0 else jnp.concatenate([s[:, :past], tail], axis=1)
                m = jnp.max(s, axis=1, keepdims=True)
                p = jnp.exp2(s - m)
                den = jnp.sum(p, axis=1, keepdims=True)
                o = jnp.dot(p.astype(BF16), v_ref[0:kw, sl], preferred_element_type=F32) / den
                o_ref[:, sl] = o.astype(o_ref.dtype)


def _moba(q, kv, heads_per_step=8, blk=MOBA_BLOCK):
    b, s, c = q.shape
    n_heads = c // HEAD_DIM
    nblk = s // blk
    assert nblk % 2 == 0 and LANES % nblk == 0 and heads_per_step * nblk <= LANES
    hw = heads_per_step * HEAD_DIM
    ng = n_heads // heads_per_step
    et = (jnp.arange(s, dtype=jnp.int32)[:, None] // blk == jnp.arange(LANES, dtype=jnp.int32)[None, :]).astype(BF16)
    kern = functools.partial(_moba_kernel, heads=heads_per_step, seq=s, blk=blk)
    return pl.pallas_call(
        kern,
        grid=(b, ng, s // blk),
        in_specs=[pl.BlockSpec((None, blk, hw), lambda bi, g, t: (bi, t, g)),
                  pl.BlockSpec((None, s, hw), lambda bi, g, t: (bi, 0, g)),
                  pl.BlockSpec((None, s, hw), lambda bi, g, t: (bi, 0, g + ng)),
                  pl.BlockSpec((s, LANES), lambda bi, g, t: (0, 0))],
        out_specs=pl.BlockSpec((None, blk, hw), lambda bi, g, t: (bi, t, g)),
        out_shape=jax.ShapeDtypeStruct((b, s, c), BF16),
        scratch_shapes=[pltpu.VMEM((LANES, hw), BF16)],
        compiler_params=_cparams(3),
        name="moba_attn",
    )(q, kv, kv, et)


def _router_kernel(h_ref, wr_ref, br_ref, idx_ref, w_ref, rank_ref, cnt_ref, *, tm):
    t = pl.program_id(0)

    @pl.when(t == 0)
    def _():
        cnt_ref[...] = jnp.zeros_like(cnt_ref)

    logits = lax.dot_general(wr_ref[...], h_ref[...], (((1,), (1,)), ((), ())),
                             preferred_element_type=F32) + br_ref[...]
    row = lax.broadcasted_iota(jnp.int32, (N_EXPERTS, tm), 0)
    rem = logits
    vals, idxs, hots = [], [], []
    for _ in range(TOP_K):
        mx = jnp.max(rem, axis=0, keepdims=True)
        ix = jnp.min(jnp.where(rem == mx, row, N_EXPERTS), axis=0, keepdims=True)
        hot = row == ix
        vals.append(mx)
        idxs.append(ix)
        hots.append(hot)
        rem = jnp.where(hot, -jnp.inf, rem)
    es = [jnp.exp(v - vals[0]) for v in vals]
    tot = es[0] + es[1] + es[2] + es[3]
    sel = jnp.zeros((N_EXPERTS, tm), F32)
    for hot in hots:
        sel = sel + hot.astype(F32)
    ri = lax.broadcasted_iota(jnp.int32, (tm, tm), 0)
    ci = lax.broadcasted_iota(jnp.int32, (tm, tm), 1)
    upper = jnp.where(ri <= ci, 1.0, 0.0).astype(BF16)
    incl = jnp.dot(sel.astype(BF16), upper, preferred_element_type=F32)
    base = cnt_ref[:, 0:1]
    rank_e = base + incl - sel
    ranks = [jnp.sum(jnp.where(hot, rank_e, 0.0), axis=0, keepdims=True) for hot in hots]
    idx_ref[...] = jnp.concatenate(idxs, axis=0)
    w_ref[...] = jnp.concatenate([e / tot for e in es], axis=0)
    rank_ref[...] = jnp.concatenate(ranks, axis=0).astype(jnp.int32)
    cnt_ref[...] = jnp.broadcast_to(base + incl[:, tm - 1:tm], cnt_ref.shape)


def _router(h_bf, w_router, b_router, tm=512):
    n_tok, d = h_bf.shape
    wr_t = w_router.T.astype(BF16)
    br = b_router.reshape(N_EXPERTS, 1).astype(F32)
    tok = lambda i: (0, i)
    const = lambda i: (0, 0)
    return pl.pallas_call(
        functools.partial(_router_kernel, tm=tm),
        grid=(n_tok // tm,),
        in_specs=[pl.BlockSpec((tm, d), lambda i: (i, 0)), pl.BlockSpec((N_EXPERTS, d), const),
                  pl.BlockSpec((N_EXPERTS, 1), const)],
        out_specs=[pl.BlockSpec((TOP_K, tm), tok), pl.BlockSpec((TOP_K, tm), tok), pl.BlockSpec((TOP_K, tm), tok),
                   pl.BlockSpec((N_EXPERTS, LANES), const)],
        out_shape=[jax.ShapeDtypeStruct((TOP_K, n_tok), jnp.int32), jax.ShapeDtypeStruct((TOP_K, n_tok), F32),
                   jax.ShapeDtypeStruct((TOP_K, n_tok), jnp.int32), jax.ShapeDtypeStruct((N_EXPERTS, LANES), F32)],
        compiler_params=_cparams(1),
        name="moe_router",
    )(h_bf, wr_t, br)


SUB = 256
SUPER = 1024
SUBS_PER_SUPER = SUPER // SUB


ROW_UNROLL = 16
ROW_GRAN = 128


def _moe_tables(counts, n_slots):
    n_super_max = N_EXPERTS + n_slots // SUPER
    n_sub_e = (counts + SUB - 1) // SUB
    xs_start = SUB * (jnp.cumsum(n_sub_e) - n_sub_e)
    n_sb_e = (n_sub_e + SUBS_PER_SUPER - 1) // SUBS_PER_SUPER
    per_e = jnp.maximum((n_sub_e + jnp.maximum(n_sb_e, 1) - 1) // jnp.maximum(n_sb_e, 1), 1)
    sb_end = jnp.cumsum(n_sb_e)
    sb_start = sb_end - n_sb_e
    total = sb_end[-1]
    g = jnp.arange(n_super_max, dtype=jnp.int32)
    gc = jnp.minimum(g, total - 1)
    e_of = jnp.minimum(jnp.searchsorted(sb_end, gc, side="right"), N_EXPERTS - 1).astype(jnp.int32)
    j_in = gc - sb_start[e_of]
    blk0 = xs_start[e_of] // SUB + per_e[e_of] * j_in
    nsub = jnp.clip(n_sub_e[e_of] - per_e[e_of] * j_in, 0, per_e[e_of])
    rows = jnp.clip(counts[e_of] - SUB * per_e[e_of] * j_in, 0, SUB * nsub)
    nsub_active = jnp.where(g < total, (rows + ROW_GRAN - 1) // ROW_GRAN, 0)
    ys_start = SUPER * sb_start
    xs_tail = xs_start + SUB * jnp.maximum(n_sub_e - 1, 0)
    i32 = lambda a: a.astype(jnp.int32)
    return dict(xs_start=i32(xs_start), ys_start=i32(ys_start), per=i32(per_e), xs_tail=i32(xs_tail),
                e_of=i32(e_of), blk0=i32(blk0), nsub=i32(nsub), nsub_active=i32(nsub_active), g_out=i32(gc))


def _slot_rows_kernel(xs_start_ref, ys_start_ref, per_ref, idx_ref, rank_ref, xr_ref, yr_ref):
    idx = idx_ref[...]
    rank = rank_ref[...]
    xs0 = jnp.zeros_like(idx)
    ys0 = jnp.zeros_like(idx)
    per = jnp.ones_like(idx)
    for e in range(N_EXPERTS):
        hit = idx == e
        xs0 = jnp.where(hit, xs_start_ref[e], xs0)
        ys0 = jnp.where(hit, ys_start_ref[e], ys0)
        per = jnp.where(hit, per_ref[e], per)
    sub = jnp.right_shift(rank, SUB.bit_length() - 1)
    q = jnp.floor((sub.astype(F32) + 0.5) / per.astype(F32)).astype(jnp.int32)
    xr_ref[...] = xs0 + rank
    yr_ref[...] = ys0 + q * SUPER + (rank - q * per * SUB)


def _slot_rows(top_idx, rank, tables, tm=2048):
    k, n_tok = top_idx.shape
    tm = min(tm, n_tok)
    blk = pl.BlockSpec((k, tm), lambda i, *_: (0, i))
    grid_spec = pltpu.PrefetchScalarGridSpec(
        num_scalar_prefetch=3, grid=(n_tok // tm,), in_specs=[blk, blk], out_specs=[blk, blk])
    return pl.pallas_call(
        _slot_rows_kernel,
        grid_spec=grid_spec,
        out_shape=[jax.ShapeDtypeStruct((k, n_tok), jnp.int32)] * 2,
        compiler_params=_cparams(1),
        name="moe_slot_rows",
    )(tables["xs_start"], tables["ys_start"], tables["per"], top_idx, rank)


def _dispatch_kernel(tail_ref, used_ref, row_ref, h_ref, xs_hbm, zbuf, hbuf, sems, zsem, *, tm, n_tiles):
    i = pl.program_id(0)
    slot = i % 2
    n_groups = tm // ROW_UNROLL

    def tail_copy(e):
        return pltpu.make_async_copy(zbuf, xs_hbm.at[pl.ds(pl.multiple_of(tail_ref[e], SUB), SUB)], zsem)

    @pl.when(i == 0)
    def _():
        zbuf[...] = jnp.zeros_like(zbuf)
        for e in range(N_EXPERTS):
            @pl.when(used_ref[e] > 0)
            def _():
                tail_copy(e).start()
        for e in range(N_EXPERTS):
            @pl.when(used_ref[e] > 0)
            def _():
                tail_copy(e).wait()

    def drain(s):
        def body(gi, carry):
            t0 = pl.multiple_of(gi * ROW_UNROLL, ROW_UNROLL)
            for u in range(ROW_UNROLL):
                for k in range(TOP_K):
                    pltpu.make_async_copy(hbuf.at[s, pl.ds(t0 + u, 1)], xs_hbm.at[pl.ds(0, 1)], sems.at[s]).wait()
            return carry
        lax.fori_loop(0, n_groups, body, 0)

    @pl.when(i >= 2)
    def _():
        drain(slot)

    hbuf[slot] = h_ref[...]

    def issue(gi, carry):
        t0 = pl.multiple_of(gi * ROW_UNROLL, ROW_UNROLL)
        for u in range(ROW_UNROLL):
            for k in range(TOP_K):
                pltpu.make_async_copy(hbuf.at[slot, pl.ds(t0 + u, 1)], xs_hbm.at[pl.ds(row_ref[k, t0 + u], 1)],
                                      sems.at[slot]).start()
        return carry

    lax.fori_loop(0, n_groups, issue, 0)

    @pl.when(i == n_tiles - 1)
    def _():
        if n_tiles >= 2:
            drain(1 - slot)
        drain(slot)


def _dispatch(hp, xs_row, tables, counts, n_rows, tm=256):
    n_tok, dp = hp.shape
    n_tiles = n_tok // tm
    grid_spec = pltpu.PrefetchScalarGridSpec(
        num_scalar_prefetch=2,
        grid=(n_tiles,),
        in_specs=[pl.BlockSpec((TOP_K, tm), lambda i, *_: (0, i), memory_space=pltpu.SMEM),
                  pl.BlockSpec((tm, dp), lambda i, *_: (i, 0))],
        out_specs=pl.BlockSpec(memory_space=pl.ANY),
        scratch_shapes=[pltpu.VMEM((SUB, dp), hp.dtype), pltpu.VMEM((2, tm, dp), hp.dtype),
                        pltpu.SemaphoreType.DMA((2,)), pltpu.SemaphoreType.DMA],
    )
    return pl.pallas_call(
        functools.partial(_dispatch_kernel, tm=tm, n_tiles=n_tiles),
        grid_spec=grid_spec,
        out_shape=jax.ShapeDtypeStruct((n_rows, dp), hp.dtype),
        compiler_params=_cparams(1),
        name="moe_dispatch",
    )(tables["xs_tail"], counts, xs_row, hp)


def _ffn_kernel(e_ref, blk0_ref, nsub_ref, act_ref, out_ref_idx, *refs, tf):
    del e_ref, blk0_ref, nsub_ref, out_ref_idx
    x_refs = refs[:SUBS_PER_SUPER]
    wg_ref, wu_ref, bg_ref, bu_ref, wd_ref, bd_ref, y_ref, xb = refs[SUBS_PER_SUPER:]
    g = pl.program_id(0)
    j = pl.program_id(1)
    nact = act_ref[g]

    @pl.when(j == 0)
    def _():
        half = xb.shape[1] // 2
        for s, x_ref in enumerate(x_refs):
            @pl.when(s * (SUB // ROW_GRAN) < nact)
            def _():
                lo, hi = _unpack_pairs(x_ref[...])
                xb[s * SUB:(s + 1) * SUB, :half] = lo
                xb[s * SUB:(s + 1) * SUB, half:] = hi

    for n in range(1, SUPER // ROW_GRAN + 1):
        @pl.when(nact == n)
        def _():
            m = n * ROW_GRAN
            x = xb[0:m, :]
            gate = jnp.dot(x, wg_ref[...].astype(BF16), preferred_element_type=F32) + bg_ref[...]
            lin = jnp.dot(x, wu_ref[...].astype(BF16), preferred_element_type=F32) + bu_ref[...]
            gate = jnp.minimum(gate, SWIGLU_LIMIT)
            lin = jnp.clip(lin, -SWIGLU_LIMIT, SWIGLU_LIMIT)
            hid = (lin + 1.0) * gate * jax.nn.sigmoid(SWIGLU_ALPHA * gate)

            @pl.when(j == 0)
            def _():
                y_ref[0:m, :] = jnp.broadcast_to(bd_ref[...], (m, y_ref.shape[1]))

            y_ref[0:m, :] += jnp.dot(hid.astype(BF16), wd_ref[...].astype(BF16), preferred_element_type=F32)


def _expert_ffn(xs, tables, layer, w_gu, b_gu, w_down, b_down, n_super_max, tf=512):
    e_of, blk0, nsub, nsub_active, g_out = (tables[k] for k in ("e_of", "blk0", "nsub", "nsub_active", "g_out"))
    n_rows, dp = xs.shape
    d = 2 * dp
    n_we = w_gu.shape[0] * N_EXPERTS
    e_of = e_of + layer * N_EXPERTS
    w_gu = w_gu.reshape(n_we, d, 2 * D_FF)
    w_down = w_down.reshape(n_we, D_FF, d)
    b_gu = b_gu.reshape(n_we, 1, 2 * D_FF)
    b_down = b_down.reshape(n_we, 1, d)
    n_ff = D_FF // tf
    last = n_ff - 1

    def x_spec(s):
        return pl.BlockSpec(
            (SUB, dp), lambda g, j, e, b0, ns, na, go: (b0[g] + jnp.minimum(s, jnp.maximum(ns[g], 1) - 1), 0))

    def jj(j, na, g):
        return jnp.where(na[g] > 0, j, last)

    in_specs = [x_spec(s) for s in range(SUBS_PER_SUPER)] + [
        pl.BlockSpec((None, d, tf), lambda g, j, e, b0, ns, na, go: (e[g], 0, jj(j, na, g))),
        pl.BlockSpec((None, d, tf), lambda g, j, e, b0, ns, na, go: (e[g], 0, n_ff + jj(j, na, g))),
        pl.BlockSpec((None, 1, tf), lambda g, j, e, b0, ns, na, go: (e[g], 0, jj(j, na, g))),
        pl.BlockSpec((None, 1, tf), lambda g, j, e, b0, ns, na, go: (e[g], 0, n_ff + jj(j, na, g))),
        pl.BlockSpec((None, tf, d), lambda g, j, e, b0, ns, na, go: (e[g], jj(j, na, g), 0)),
        pl.BlockSpec((None, 1, d), lambda g, j, e, b0, ns, na, go: (e[g], 0, 0)),
    ]
    grid_spec = pltpu.PrefetchScalarGridSpec(
        num_scalar_prefetch=5,
        grid=(n_super_max, n_ff),
        in_specs=in_specs,
        out_specs=pl.BlockSpec((SUPER, d), lambda g, j, e, b0, ns, na, go: (go[g], 0)),
        scratch_shapes=[pltpu.VMEM((SUPER, d), BF16)],
    )
    return pl.pallas_call(
        functools.partial(_ffn_kernel, tf=tf),
        grid_spec=grid_spec,
        out_shape=jax.ShapeDtypeStruct((n_super_max * SUPER, d), F32),
        compiler_params=_cparams(2, FFN_VMEM_LIMIT),
        name="moe_ffn",
    )(e_of, blk0, nsub, nsub_active, g_out, *([xs] * SUBS_PER_SUPER), w_gu, w_gu, b_gu, b_gu, w_down, b_down)


def _combine_kernel(row_ref, next_ref, w_ref, h_ref, g_ref, b_ref, ys_hbm, of_ref, ob_ref, op_ref, ybuf, sems,
                    *, tm, n_tiles):
    i = pl.program_id(0)
    slot = i % 2
    n_groups = tm // ROW_UNROLL

    def issue_group(rows_smem, s, gi):
        t0 = pl.multiple_of(gi * ROW_UNROLL, ROW_UNROLL)
        for u in range(ROW_UNROLL):
            for k in range(TOP_K):
                pltpu.make_async_copy(ys_hbm.at[pl.ds(rows_smem[k, t0 + u], 1)],
                                      ybuf.at[s, k, pl.ds(t0 + u, 1)], sems.at[s]).start()

    def wait_group(gi, carry):
        t0 = pl.multiple_of(gi * ROW_UNROLL, ROW_UNROLL)
        for u in range(ROW_UNROLL):
            for k in range(TOP_K):
                pltpu.make_async_copy(ys_hbm.at[pl.ds(0, 1)], ybuf.at[slot, k, pl.ds(t0 + u, 1)],
                                      sems.at[slot]).wait()
        return carry

    @pl.when(i == 0)
    def _():
        def first(gi, carry):
            issue_group(row_ref, 0, gi)
            return carry
        lax.fori_loop(0, n_groups, first, 0)

    lax.fori_loop(0, n_groups, wait_group, 0)

    @pl.when(i + 1 < n_tiles)
    def _():
        def ahead(gi, carry):
            issue_group(next_ref, 1 - slot, gi)
            return carry
        lax.fori_loop(0, n_groups, ahead, 0)

    w = w_ref[...]
    ffn = w[:, 0:1] * ybuf[slot, 0]
    for k in range(1, TOP_K):
        ffn = ffn + w[:, k:k + 1] * ybuf[slot, k]
    y = _layer_norm_rows(DEEPNORM_ALPHA * h_ref[...] + ffn, g_ref[...], b_ref[...])
    of_ref[...] = y
    ob_ref[...] = y.astype(ob_ref.dtype)
    op_ref[...] = _pack_pairs(y)


def _combine_ln(ys, ys_row, w_t, h, g, b, tm=128):
    n_tok, d = h.shape
    n_tiles = n_tok // tm
    row = lambda i: (i, 0)
    const = lambda i: (0, 0)
    return pl.pallas_call(
        functools.partial(_combine_kernel, tm=tm, n_tiles=n_tiles),
        grid=(n_tiles,),
        in_specs=[pl.BlockSpec((TOP_K, tm), lambda i: (0, i), memory_space=pltpu.SMEM),
                  pl.BlockSpec((TOP_K, tm), lambda i: (0, jnp.minimum(i + 1, n_tiles - 1)), memory_space=pltpu.SMEM),
                  pl.BlockSpec((tm, TOP_K), row), pl.BlockSpec((tm, d), row),
                  pl.BlockSpec((1, d), const), pl.BlockSpec((1, d), const), pl.BlockSpec(memory_space=pl.ANY)],
        out_specs=[pl.BlockSpec((tm, d), row), pl.BlockSpec((tm, d), row), pl.BlockSpec((tm, d // 2), row)],
        out_shape=[jax.ShapeDtypeStruct((n_tok, d), F32), jax.ShapeDtypeStruct((n_tok, d), BF16),
                   jax.ShapeDtypeStruct((n_tok, d // 2), jnp.uint32)],
        scratch_shapes=[pltpu.VMEM((2, TOP_K, tm, d), F32), pltpu.SemaphoreType.DMA((2,))],
        compiler_params=_cparams(1),
        name="moe_combine_ln",
    )(ys_row, ys_row, w_t, h, g.reshape(1, d), b.reshape(1, d), ys)


def _moe_ln(h, h_bf, h_pk, layer, w_router, b_router, w_gu, b_gu, w_down, b_down, ln_g, ln_b):
    n_tok, d = h.shape
    n_slots = n_tok * TOP_K
    top_idx, top_w, rank, cnt = _router(h_bf, w_router, b_router)
    counts = cnt[:, 0].astype(jnp.int32)
    tables = _moe_tables(counts, n_slots)
    xs_row, ys_row = _slot_rows(top_idx, rank, tables)
    n_super_max = N_EXPERTS + n_slots // SUPER
    n_rows = n_slots + N_EXPERTS * SUB
    xs = _dispatch(h_pk, xs_row, tables, counts, n_rows)
    ys = _expert_ffn(xs, tables, layer, w_gu, b_gu, w_down, b_down, n_super_max)
    return _combine_ln(ys, ys_row, top_w.T, h, ln_g, ln_b)


def _dilated_mix(x3, w_qkv, tabs):
    b, s, d = x3.shape
    n_tok = b * s
    width = A_HEADS * HEAD_DIM
    outs, lses = [], []
    for g, (window, dil) in enumerate(A_PATTERNS):
        assert window // dil == 128
        L = s // dil
        if dil == 1:
            xp = x3.reshape(n_tok, d)
        else:
            xp = x3.reshape(b, L, dil, d).transpose(0, 2, 1, 3).reshape(n_tok, d).astype(BF16)
        tabs_p = tabs.reshape(3, L, dil, HEAD_DIM).transpose(0, 2, 1, 3).reshape(3, s, HEAD_DIM)
        qkv = _proj(xp, w_qkv, tabs_p, col0=g * 3 * width, n_out=3 * width, pos_period=s, rope_mod=3, rope_cnt=2)
        o, lse = _band_attn(qkv.reshape(b * dil, L, 3 * width))
        outs.append(o.reshape(b, dil, L, width).transpose(0, 2, 1, 3).reshape(n_tok, width))
        lses.append(lse.reshape(b, dil, L, A_HEADS).transpose(0, 2, 1, 3).reshape(n_tok, A_HEADS))
    return _mix(outs, lses)


def kernel(x, a_w_qkv, a_w_o, kv_w, b_w_q, b_w_o, router_w, router_b, moe_w_gate_up, moe_b_gate_up, moe_w_down,
           moe_b_down, ln1_g, ln1_b, ln2_g, ln2_b):
    b, s, d = x.shape
    n_tok = b * s
    tabs = _rope_tables(s)
    h = x.reshape(n_tok, d)
    h_bf = h.astype(BF16)
    kv = None
    for layer in range(DEPTH):
        if layer < N_A_LAYERS:
            mix = _dilated_mix(h.reshape(b, s, d), a_w_qkv[layer], tabs)
            w_o = a_w_o[layer]
        else:
            j = layer - N_A_LAYERS
            if layer == N_A_LAYERS:
                kv = _proj(h_bf, kv_w, tabs, col0=0, n_out=2 * B_HEADS * HEAD_DIM, pos_period=s, rope_mod=2,
                           rope_cnt=1)
            q = _proj(h_bf, b_w_q[j], tabs, col0=0, n_out=B_HEADS * HEAD_DIM, pos_period=s, rope_mod=1, rope_cnt=1)
            mix = _moba(q.reshape(b, s, -1), kv.reshape(b, s, -1)).reshape(n_tok, -1)
            w_o = b_w_o[j]
        h, h_bf, h_pk = _oproj_ln(mix, w_o.astype(BF16), h, ln1_g[layer], ln1_b[layer])
        h, h_bf, _ = _moe_ln(h, h_bf, h_pk, layer, router_w[layer], router_b[layer], moe_w_gate_up, moe_b_gate_up,
                             moe_w_down, moe_b_down, ln2_g[layer], ln2_b[layer])
    return h.reshape(b, s, d)
```

```python
import functools

import jax
import jax.numpy as jnp
from jax import lax
from jax.experimental import pallas as pl
from jax.experimental.pallas import tpu as pltpu

D_MODEL = 2048
SEQ = 2048
DEPTH = 2
HEAD_DIM = 128
ROT_DIM = HEAD_DIM // 4
ROPE_THETA = 500000.0
A_PATTERNS = ((128, 1), (512, 4), (2048, 16))
A_GROUPS = len(A_PATTERNS)
A_HEADS = 16
B_HEADS = 16
MOBA_BLOCK = 256
MOBA_TOPK = 3
N_EXPERTS = 32
TOP_K = 4
D_FF = 2048
SWIGLU_LIMIT = 7.0
SWIGLU_ALPHA = 1.702
N_A_LAYERS = DEPTH // 2
DEEPNORM_ALPHA = (2 * DEPTH) ** 0.25
LN_EPS = 1e-5
NEG_INF = -1e30
LOG2_E = 1.4426950408889634
LN_2 = 0.6931471805599453

LANES = 128
V7X_VMEM_BYTES = 64 * 1024 * 1024
VMEM_LIMIT = 56 * 1024 * 1024
FFN_VMEM_LIMIT = 61 * 1024 * 1024

BF16 = jnp.bfloat16
F32 = jnp.float32


def _cparams(n_axes, vmem_limit=VMEM_LIMIT):
    return pltpu.CompilerParams(dimension_semantics=("arbitrary",) * n_axes, vmem_limit_bytes=vmem_limit)


def _rope_tables(seq):
    half = ROT_DIM // 2
    inv = ROPE_THETA ** (-jnp.arange(0, ROT_DIM, 2, dtype=F32) / ROT_DIM)
    ang = jnp.arange(seq, dtype=F32)[:, None] * inv[None, :]
    cos, sin = jnp.cos(ang), jnp.sin(ang)
    zeros = jnp.zeros((seq, HEAD_DIM - ROT_DIM), F32)
    c = jnp.concatenate([cos, cos, jnp.ones((seq, HEAD_DIM - ROT_DIM), F32)], axis=1)
    s1 = jnp.concatenate([-sin, jnp.zeros((seq, half), F32), zeros], axis=1)
    s2 = jnp.concatenate([jnp.zeros((seq, half), F32), sin, zeros], axis=1)
    return jnp.stack([c, s1, s2])


def _rope_tile(x, c, s1, s2):
    half = ROT_DIM // 2
    return x * c + pltpu.roll(x, HEAD_DIM - half, 1) * s1 + pltpu.roll(x, half, 1) * s2


def _proj_kernel(a_ref, w_ref, tab_ref, o_ref, acc_s, *abf, tn, nj, n_tiles):
    t = pl.program_id(0)

    @pl.when(t == 0)
    def _():
        acc_s[...] = jnp.zeros_like(acc_s)

    if abf:
        @pl.when(jnp.minimum(t, n_tiles - 1) % nj == 0)
        def _():
            abf[0][...] = a_ref[...].astype(BF16)
        a_src = abf[0]
    else:
        a_src = a_ref

    prev = acc_s[...]
    c, s1, s2 = tab_ref[0], tab_ref[1], tab_ref[2]
    for h in range(tn // HEAD_DIM):
        sl = slice(h * HEAD_DIM, (h + 1) * HEAD_DIM)
        o_ref[:, sl] = _rope_tile(prev[:, sl], c, s1, s2).astype(o_ref.dtype)
    acc_s[...] = jnp.dot(a_src[...], w_ref[...].astype(BF16), preferred_element_type=F32)


def _proj(a, w, tabs, *, col0, n_out, pos_period, rope_mod, rope_cnt, tm=1024, tn=1024):
    m, k = a.shape
    assert m % tm == 0 and n_out % tn == 0 and col0 % tn == 0 and pos_period % tm == 0 and D_MODEL % tn == 0
    jb = col0 // tn
    per = pos_period // tm
    ident = jnp.stack([jnp.ones_like(tabs[0]), jnp.zeros_like(tabs[0]), jnp.zeros_like(tabs[0])])
    tabs2 = jnp.stack([tabs, ident])

    nj = n_out // tn
    n_tiles = (m // tm) * nj

    def plain(j):
        return jnp.where(((j * tn) // D_MODEL) % rope_mod < rope_cnt, 0, 1)

    cur = lambda t: jnp.minimum(t, n_tiles - 1)
    fin = lambda t: jnp.maximum(t - 1, 0)
    scratch = [pltpu.VMEM((tm, tn), F32)]
    if a.dtype != BF16:
        assert nj > 1
        scratch.append(pltpu.VMEM((tm, k), BF16))
    return pl.pallas_call(
        functools.partial(_proj_kernel, tn=tn, nj=nj, n_tiles=n_tiles),
        grid=(n_tiles + 1,),
        in_specs=[
            pl.BlockSpec((tm, k), lambda t: (cur(t) // nj, 0)),
            pl.BlockSpec((k, tn), lambda t: (0, cur(t) % nj + jb)),
            pl.BlockSpec((None, 3, tm, HEAD_DIM), lambda t: (plain(fin(t) % nj), 0, (fin(t) // nj) % per, 0)),
        ],
        out_specs=pl.BlockSpec((tm, tn), lambda t: (fin(t) // nj, fin(t) % nj)),
        out_shape=jax.ShapeDtypeStruct((m, n_out), BF16),
        scratch_shapes=scratch,
        compiler_params=_cparams(1),
        name="proj_rope",
    )(a, w, tabs2)


def _band_attn_kernel(*refs, has_prev, n_heads, blk):
    if has_prev:
        q_ref, kp_ref, kc_ref, vp_ref, vc_ref, o_ref, lse_ref = refs
    else:
        q_ref, kc_ref, vc_ref, o_ref, lse_ref = refs
    i = pl.program_id(1)
    nk = 2 * blk if has_prev else blk
    qi = lax.broadcasted_iota(jnp.int32, (blk, nk), 0)
    ki = lax.broadcasted_iota(jnp.int32, (blk, nk), 1) - (nk - blk)
    diff = qi - ki
    mask = (diff >= 0) & (diff <= blk)
    if has_prev:
        mask = mask & ((i > 0) | (ki >= 0))
    scale = HEAD_DIM ** -0.5
    lane = lax.broadcasted_iota(jnp.int32, (blk, n_heads), 1)
    lse_all = jnp.zeros((blk, n_heads), F32)
    for h in range(n_heads):
        sl = slice(h * HEAD_DIM, (h + 1) * HEAD_DIM)
        q = q_ref[:, sl]
        if has_prev:
            k = jnp.concatenate([kp_ref[:, sl], kc_ref[:, sl]], axis=0)
            v = jnp.concatenate([vp_ref[:, sl], vc_ref[:, sl]], axis=0)
        else:
            k, v = kc_ref[:, sl], vc_ref[:, sl]
        s = lax.dot_general(q, k, (((1,), (1,)), ((), ())), preferred_element_type=F32) * (scale * LOG2_E)
        s = jnp.where(mask, s, NEG_INF)
        m = jnp.max(s, axis=1, keepdims=True)
        p = jnp.exp2(s - m)
        den = jnp.sum(p, axis=1, keepdims=True)
        o = jnp.dot(p.astype(BF16), v, preferred_element_type=F32) / den
        o_ref[:, sl] = o.astype(o_ref.dtype)
        lse_all = jnp.where(lane == h, m * LN_2 + jnp.log(den), lse_all)
    lse_ref[...] = lse_all


def _band_attn(qkv, n_heads=A_HEADS, blk=128):
    n, L, c3 = qkv.shape
    c = c3 // 3
    assert c == n_heads * HEAD_DIM and L % blk == 0
    nb = L // blk
    has_prev = True
    spec = lambda col, prev: pl.BlockSpec(
        (None, blk, c), (lambda s, i: (s, jnp.maximum(i - 1, 0), col)) if prev else (lambda s, i: (s, i, col)))
    if has_prev:
        in_specs = [spec(0, False), spec(1, True), spec(1, False), spec(2, True), spec(2, False)]
        args = (qkv,) * 5
    else:
        in_specs = [spec(0, False), spec(1, False), spec(2, False)]
        args = (qkv,) * 3
    kern = functools.partial(_band_attn_kernel, has_prev=has_prev, n_heads=n_heads, blk=blk)
    return pl.pallas_call(
        kern,
        grid=(n, nb),
        in_specs=in_specs,
        out_specs=[pl.BlockSpec((None, blk, c), lambda s, i: (s, i, 0)),
                   pl.BlockSpec((None, blk, n_heads), lambda s, i: (s, i, 0))],
        out_shape=[jax.ShapeDtypeStruct((n, L, c), BF16), jax.ShapeDtypeStruct((n, L, n_heads), F32)],
        compiler_params=_cparams(2),
        name="band_attn",
    )(*args)


def _mix_kernel(o0, o1, o2, l0, l1, l2, out_ref, *, n_heads):
    ls = [l0[...], l1[...], l2[...]]
    mx = jnp.maximum(jnp.maximum(ls[0], ls[1]), ls[2])
    es = [jnp.exp(l - mx) for l in ls]
    tot = es[0] + es[1] + es[2]
    ws = [e / tot for e in es]
    os_ = [o0, o1, o2]
    for h in range(n_heads):
        sl = slice(h * HEAD_DIM, (h + 1) * HEAD_DIM)
        acc = ws[0][:, h:h + 1] * os_[0][:, sl].astype(F32)
        acc += ws[1][:, h:h + 1] * os_[1][:, sl].astype(F32)
        acc += ws[2][:, h:h + 1] * os_[2][:, sl].astype(F32)
        out_ref[:, sl] = acc.astype(out_ref.dtype)


def _mix(os_, ls, tm=256, n_heads=A_HEADS):
    m, c = os_[0].shape
    ospec = pl.BlockSpec((tm, c), lambda i: (i, 0))
    lspec = pl.BlockSpec((tm, n_heads), lambda i: (i, 0))
    return pl.pallas_call(
        functools.partial(_mix_kernel, n_heads=n_heads),
        grid=(m // tm,),
        in_specs=[ospec] * 3 + [lspec] * 3,
        out_specs=ospec,
        out_shape=jax.ShapeDtypeStruct((m, c), BF16),
        compiler_params=_cparams(1),
        name="group_mix",
    )(*os_, *ls)


def _layer_norm_rows(y, g, b):
    mu = jnp.mean(y, axis=1, keepdims=True)
    yc = y - mu
    var = jnp.mean(yc * yc, axis=1, keepdims=True)
    return yc * lax.rsqrt(var + LN_EPS) * g + b


def _pack_pairs(y):
    half = y.shape[1] // 2
    lo = lax.bitcast_convert_type(y[:, :half].astype(BF16).astype(F32), jnp.uint32)
    hi = lax.bitcast_convert_type(y[:, half:].astype(BF16).astype(F32), jnp.uint32)
    return hi | (lo >> 16)


def _unpack_pairs(u):
    lo = lax.bitcast_convert_type(u << 16, F32).astype(BF16)
    hi = lax.bitcast_convert_type(u & jnp.uint32(0xFFFF0000), F32).astype(BF16)
    return lo, hi


def _oproj_ln_kernel(a_ref, w_ref, h_ref, g_ref, b_ref, of_ref, ob_ref, op_ref, acc_s):
    @pl.when(pl.program_id(0) == 0)
    def _():
        acc_s[...] = jnp.zeros_like(acc_s)

    y = _layer_norm_rows(DEEPNORM_ALPHA * h_ref[...] + acc_s[...], g_ref[...], b_ref[...])
    of_ref[...] = y
    ob_ref[...] = y.astype(ob_ref.dtype)
    op_ref[...] = _pack_pairs(y)
    acc_s[...] = jnp.dot(a_ref[...], w_ref[...], preferred_element_type=F32)


def _oproj_ln(a, w_bf, h, g, b, tm=512):
    m, k = a.shape
    d = w_bf.shape[1]
    row = lambda i: (i, 0)
    const = lambda i: (0, 0)
    n_tiles = m // tm
    cur = lambda t: (jnp.minimum(t, n_tiles - 1), 0)
    fin = lambda t: (jnp.maximum(t - 1, 0), 0)
    return pl.pallas_call(
        _oproj_ln_kernel,
        grid=(n_tiles + 1,),
        in_specs=[pl.BlockSpec((tm, k), cur), pl.BlockSpec((k, d), const), pl.BlockSpec((tm, d), fin),
                  pl.BlockSpec((1, d), const), pl.BlockSpec((1, d), const)],
        out_specs=[pl.BlockSpec((tm, d), fin), pl.BlockSpec((tm, d), fin), pl.BlockSpec((tm, d // 2), fin)],
        out_shape=[jax.ShapeDtypeStruct((m, d), F32), jax.ShapeDtypeStruct((m, d), BF16),
                   jax.ShapeDtypeStruct((m, d // 2), jnp.uint32)],
        scratch_shapes=[pltpu.VMEM((tm, d), F32)],
        compiler_params=_cparams(1),
        name="oproj_ln",
    )(a, w_bf, h, g.reshape(1, d), b.reshape(1, d))


def _moba_kernel(q_ref, k_ref, v_ref, et_ref, o_ref, km_s, *, heads, seq, blk):
    c = pl.program_id(2)
    nblk = seq // blk
    hw = heads * HEAD_DIM
    scale = HEAD_DIM ** -0.5

    @pl.when(c == 0)
    def _():
        ar = lax.broadcasted_iota(jnp.int32, (16, seq), 0)
        ac = lax.broadcasted_iota(jnp.int32, (16, seq), 1)
        avg = jnp.where(ac // blk == ar, 1.0 / blk, 0.0).astype(BF16)
        kmean = jnp.dot(avg, k_ref[...], preferred_element_type=F32)[0:nblk]
        kmt = jnp.concatenate([kmean] * (LANES // nblk), axis=0)
        kr = lax.broadcasted_iota(jnp.int32, (LANES, hw), 0)
        kc = lax.broadcasted_iota(jnp.int32, (LANES, hw), 1)
        km_s[...] = jnp.where(kr // nblk == kc // HEAD_DIM, kmt, 0.0).astype(BF16)

    gate = lax.dot_general(q_ref[...], km_s[...], (((1,), (1,)), ((), ())),
                           preferred_element_type=F32)
    lane = lax.broadcasted_iota(jnp.int32, (blk, LANES), 1)
    n = lane % nblk
    valid = (n < c) & (lane < heads * nblk)
    g = jnp.where(valid, gate, NEG_INF)
    cnt = jnp.zeros((blk, LANES), jnp.int32)
    for sh in range(1, nblk):
        lo = pltpu.roll(g, sh, 1)
        cnt = cnt + jnp.where((n >= sh) & (lo >= g), 1, 0)
        hi = pltpu.roll(g, LANES - sh, 1)
        cnt = cnt + jnp.where((n + sh < nblk) & (hi > g), 1, 0)
    sel = (cnt < MOBA_TOPK) & valid
    bias = jnp.where(sel | (n == c), 0.0, NEG_INF).astype(F32)

    for nb in range(2, nblk + 1, 2):
        @pl.when(2 * (c // 2 + 1) == nb)
        def _():
            kw = nb * blk
            past = kw - 2 * blk
            qpos = c * blk + lax.broadcasted_iota(jnp.int32, (blk, 2 * blk), 0)
            kpos = past + lax.broadcasted_iota(jnp.int32, (blk, 2 * blk), 1)
            causal = kpos <= qpos
            et = et_ref[0:kw, :]
            for h in range(heads):
                sl = slice(h * HEAD_DIM, (h + 1) * HEAD_DIM)
                bias_h = bias if h == 0 else pltpu.roll(bias, LANES - h * nblk, 1)
                q_aug = jnp.concatenate([q_ref[:, sl], bias_h.astype(BF16)], axis=1)
                k_aug = jnp.concatenate([k_ref[0:kw, sl], et], axis=1)
                s = lax.dot_general(q_aug, k_aug, (((1,), (1,)), ((), ())),
                                    preferred_element_type=F32) * (scale * LOG2_E)
                tail = jnp.where(causal, s[:, past:], NEG_INF)
                s = tail if past == 0 else jnp.concatenate([s[:, :past], tail], axis=1)
                m = jnp.max(s, axis=1, keepdims=True)
                p = jnp.exp2(s - m)
                den = jnp.sum(p, axis=1, keepdims=True)
                o = jnp.dot(p.astype(BF16), v_ref[0:kw, sl], preferred_element_type=F32) / den
                o_ref[:, sl] = o.astype(o_ref.dtype)


def _moba(q, kv, heads_per_step=8, blk=MOBA_BLOCK):
    b, s, c = q.shape
    n_heads = c // HEAD_DIM
    nblk = s // blk
    assert nblk % 2 == 0 and LANES % nblk == 0 and heads_per_step * nblk <= LANES
    hw = heads_per_step * HEAD_DIM
    ng = n_heads // heads_per_step
    et = (jnp.arange(s, dtype=jnp.int32)[:, None] // blk == jnp.arange(LANES, dtype=jnp.int32)[None, :]).astype(BF16)
    kern = functools.partial(_moba_kernel, heads=heads_per_step, seq=s, blk=blk)
    return pl.pallas_call(
        kern,
        grid=(b, ng, s // blk),
        in_specs=[pl.BlockSpec((None, blk, hw), lambda bi, g, t: (bi, t, g)),
                  pl.BlockSpec((None, s, hw), lambda bi, g, t: (bi, 0, g)),
                  pl.BlockSpec((None, s, hw), lambda bi, g, t: (bi, 0, g + ng)),
                  pl.BlockSpec((s, LANES), lambda bi, g, t: (0, 0))],
        out_specs=pl.BlockSpec((None, blk, hw), lambda bi, g, t: (bi, t, g)),
        out_shape=jax.ShapeDtypeStruct((b, s, c), BF16),
        scratch_shapes=[pltpu.VMEM((LANES, hw), BF16)],
        compiler_params=_cparams(3),
        name="moba_attn",
    )(q, kv, kv, et)


def _router_kernel(h_ref, wr_ref, br_ref, idx_ref, w_ref, rank_ref, cnt_ref, *, tm):
    t = pl.program_id(0)

    @pl.when(t == 0)
    def _():
        cnt_ref[...] = jnp.zeros_like(cnt_ref)

    logits = lax.dot_general(wr_ref[...], h_ref[...], (((1,), (1,)), ((), ())),
                             preferred_element_type=F32) + br_ref[...]
    row = lax.broadcasted_iota(jnp.int32, (N_EXPERTS, tm), 0)
    rem = logits
    vals, idxs, hots = [], [], []
    for _ in range(TOP_K):
        mx = jnp.max(rem, axis=0, keepdims=True)
        ix = jnp.min(jnp.where(rem == mx, row, N_EXPERTS), axis=0, keepdims=True)
        hot = row == ix
        vals.append(mx)
        idxs.append(ix)
        hots.append(hot)
        rem = jnp.where(hot, -jnp.inf, rem)
    es = [jnp.exp(v - vals[0]) for v in vals]
    tot = es[0] + es[1] + es[2] + es[3]
    sel = jnp.zeros((N_EXPERTS, tm), F32)
    for hot in hots:
        sel = sel + hot.astype(F32)
    ri = lax.broadcasted_iota(jnp.int32, (tm, tm), 0)
    ci = lax.broadcasted_iota(jnp.int32, (tm, tm), 1)
    upper = jnp.where(ri <= ci, 1.0, 0.0).astype(BF16)
    incl = jnp.dot(sel.astype(BF16), upper, preferred_element_type=F32)
    base = cnt_ref[:, 0:1]
    rank_e = base + incl - sel
    ranks = [jnp.sum(jnp.where(hot, rank_e, 0.0), axis=0, keepdims=True) for hot in hots]
    idx_ref[...] = jnp.concatenate(idxs, axis=0)
    w_ref[...] = jnp.concatenate([e / tot for e in es], axis=0)
    rank_ref[...] = jnp.concatenate(ranks, axis=0).astype(jnp.int32)
    cnt_ref[...] = jnp.broadcast_to(base + incl[:, tm - 1:tm], cnt_ref.shape)


def _router(h_bf, w_router, b_router, tm=512):
    n_tok, d = h_bf.shape
    wr_t = w_router.T.astype(BF16)
    br = b_router.reshape(N_EXPERTS, 1).astype(F32)
    tok = lambda i: (0, i)
    const = lambda i: (0, 0)
    return pl.pallas_call(
        functools.partial(_router_kernel, tm=tm),
        grid=(n_tok // tm,),
        in_specs=[pl.BlockSpec((tm, d), lambda i: (i, 0)), pl.BlockSpec((N_EXPERTS, d), const),
                  pl.BlockSpec((N_EXPERTS, 1), const)],
        out_specs=[pl.BlockSpec((TOP_K, tm), tok), pl.BlockSpec((TOP_K, tm), tok), pl.BlockSpec((TOP_K, tm), tok),
                   pl.BlockSpec((N_EXPERTS, LANES), const)],
        out_shape=[jax.ShapeDtypeStruct((TOP_K, n_tok), jnp.int32), jax.ShapeDtypeStruct((TOP_K, n_tok), F32),
                   jax.ShapeDtypeStruct((TOP_K, n_tok), jnp.int32), jax.ShapeDtypeStruct((N_EXPERTS, LANES), F32)],
        compiler_params=_cparams(1),
        name="moe_router",
    )(h_bf, wr_t, br)


SUB = 256
SUPER = 1024
SUBS_PER_SUPER = SUPER // SUB


ROW_UNROLL = 16
ROW_GRAN = 128


def _moe_tables(counts, n_slots):
    n_super_max = N_EXPERTS + n_slots // SUPER
    n_sub_e = (counts + SUB - 1) // SUB
    xs_start = SUB * (jnp.cumsum(n_sub_e) - n_sub_e)
    n_sb_e = (n_sub_e + SUBS_PER_SUPER - 1) // SUBS_PER_SUPER
    per_e = jnp.maximum((n_sub_e + jnp.maximum(n_sb_e, 1) - 1) // jnp.maximum(n_sb_e, 1), 1)
    sb_end = jnp.cumsum(n_sb_e)
    sb_start = sb_end - n_sb_e
    total = sb_end[-1]
    g = jnp.arange(n_super_max, dtype=jnp.int32)
    gc = jnp.minimum(g, total - 1)
    e_of = jnp.minimum(jnp.searchsorted(sb_end, gc, side="right"), N_EXPERTS - 1).astype(jnp.int32)
    j_in = gc - sb_start[e_of]
    blk0 = xs_start[e_of] // SUB + per_e[e_of] * j_in
    nsub = jnp.clip(n_sub_e[e_of] - per_e[e_of] * j_in, 0, per_e[e_of])
    rows = jnp.clip(counts[e_of] - SUB * per_e[e_of] * j_in, 0, SUB * nsub)
    nsub_active = jnp.where(g < total, (rows + ROW_GRAN - 1) // ROW_GRAN, 0)
    ys_start = SUPER * sb_start
    xs_tail = xs_start + SUB * jnp.maximum(n_sub_e - 1, 0)
    i32 = lambda a: a.astype(jnp.int32)
    return dict(xs_start=i32(xs_start), ys_start=i32(ys_start), per=i32(per_e), xs_tail=i32(xs_tail),
                e_of=i32(e_of), blk0=i32(blk0), nsub=i32(nsub), nsub_active=i32(nsub_active), g_out=i32(gc))


def _slot_rows_kernel(xs_start_ref, ys_start_ref, per_ref, idx_ref, rank_ref, xr_ref, yr_ref):
    idx = idx_ref[...]
    rank = rank_ref[...]
    xs0 = jnp.zeros_like(idx)
    ys0 = jnp.zeros_like(idx)
    per = jnp.ones_like(idx)
    for e in range(N_EXPERTS):
        hit = idx == e
        xs0 = jnp.where(hit, xs_start_ref[e], xs0)
        ys0 = jnp.where(hit, ys_start_ref[e], ys0)
        per = jnp.where(hit, per_ref[e], per)
    sub = jnp.right_shift(rank, SUB.bit_length() - 1)
    q = jnp.floor((sub.astype(F32) + 0.5) / per.astype(F32)).astype(jnp.int32)
    xr_ref[...] = xs0 + rank
    yr_ref[...] = ys0 + q * SUPER + (rank - q * per * SUB)


def _slot_rows(top_idx, rank, tables, tm=2048):
    k, n_tok = top_idx.shape
    tm = min(tm, n_tok)
    blk = pl.BlockSpec((k, tm), lambda i, *_: (0, i))
    grid_spec = pltpu.PrefetchScalarGridSpec(
        num_scalar_prefetch=3, grid=(n_tok // tm,), in_specs=[blk, blk], out_specs=[blk, blk])
    return pl.pallas_call(
        _slot_rows_kernel,
        grid_spec=grid_spec,
        out_shape=[jax.ShapeDtypeStruct((k, n_tok), jnp.int32)] * 2,
        compiler_params=_cparams(1),
        name="moe_slot_rows",
    )(tables["xs_start"], tables["ys_start"], tables["per"], top_idx, rank)


def _dispatch_kernel(tail_ref, used_ref, row_ref, h_ref, xs_hbm, zbuf, hbuf, sems, zsem, *, tm, n_tiles):
    i = pl.program_id(0)
    slot = i % 2
    n_groups = tm // ROW_UNROLL

    def tail_copy(e):
        return pltpu.make_async_copy(zbuf, xs_hbm.at[pl.ds(pl.multiple_of(tail_ref[e], SUB), SUB)], zsem)

    @pl.when(i == 0)
    def _():
        zbuf[...] = jnp.zeros_like(zbuf)
        for e in range(N_EXPERTS):
            @pl.when(used_ref[e] > 0)
            def _():
                tail_copy(e).start()
        for e in range(N_EXPERTS):
            @pl.when(used_ref[e] > 0)
            def _():
                tail_copy(e).wait()

    def drain(s):
        def body(gi, carry):
            t0 = pl.multiple_of(gi * ROW_UNROLL, ROW_UNROLL)
            for u in range(ROW_UNROLL):
                for k in range(TOP_K):
                    pltpu.make_async_copy(hbuf.at[s, pl.ds(t0 + u, 1)], xs_hbm.at[pl.ds(0, 1)], sems.at[s]).wait()
            return carry
        lax.fori_loop(0, n_groups, body, 0)

    @pl.when(i >= 2)
    def _():
        drain(slot)

    hbuf[slot] = h_ref[...]

    def issue(gi, carry):
        t0 = pl.multiple_of(gi * ROW_UNROLL, ROW_UNROLL)
        for u in range(ROW_UNROLL):
            for k in range(TOP_K):
                pltpu.make_async_copy(hbuf.at[slot, pl.ds(t0 + u, 1)], xs_hbm.at[pl.ds(row_ref[k, t0 + u], 1)],
                                      sems.at[slot]).start()
        return carry

    lax.fori_loop(0, n_groups, issue, 0)

    @pl.when(i == n_tiles - 1)
    def _():
        if n_tiles >= 2:
            drain(1 - slot)
        drain(slot)


def _dispatch(hp, xs_row, tables, counts, n_rows, tm=512):
    n_tok, dp = hp.shape
    n_tiles = n_tok // tm
    grid_spec = pltpu.PrefetchScalarGridSpec(
        num_scalar_prefetch=2,
        grid=(n_tiles,),
        in_specs=[pl.BlockSpec((TOP_K, tm), lambda i, *_: (0, i), memory_space=pltpu.SMEM),
                  pl.BlockSpec((tm, dp), lambda i, *_: (i, 0))],
        out_specs=pl.BlockSpec(memory_space=pl.ANY),
        scratch_shapes=[pltpu.VMEM((SUB, dp), hp.dtype), pltpu.VMEM((2, tm, dp), hp.dtype),
                        pltpu.SemaphoreType.DMA((2,)), pltpu.SemaphoreType.DMA],
    )
    return pl.pallas_call(
        functools.partial(_dispatch_kernel, tm=tm, n_tiles=n_tiles),
        grid_spec=grid_spec,
        out_shape=jax.ShapeDtypeStruct((n_rows, dp), hp.dtype),
        compiler_params=_cparams(1),
        name="moe_dispatch",
    )(tables["xs_tail"], counts, xs_row, hp)


def _ffn_kernel(e_ref, blk0_ref, nsub_ref, act_ref, out_ref_idx, *refs, tf):
    del e_ref, blk0_ref, nsub_ref, out_ref_idx
    x_refs = refs[:SUBS_PER_SUPER]
    wg_ref, wu_ref, bg_ref, bu_ref, wd_ref, bd_ref, y_ref, xb = refs[SUBS_PER_SUPER:]
    g = pl.program_id(0)
    j = pl.program_id(1)
    nact = act_ref[g]

    @pl.when(j == 0)
    def _():
        half = xb.shape[1] // 2
        for s, x_ref in enumerate(x_refs):
            @pl.when(s * (SUB // ROW_GRAN) < nact)
            def _():
                lo, hi = _unpack_pairs(x_ref[...])
                xb[s * SUB:(s + 1) * SUB, :half] = lo
                xb[s * SUB:(s + 1) * SUB, half:] = hi

    for n in range(1, SUPER // ROW_GRAN + 1):
        @pl.when(nact == n)
        def _():
            m = n * ROW_GRAN
            x = xb[0:m, :]
            gate = jnp.dot(x, wg_ref[...].astype(BF16), preferred_element_type=F32) + bg_ref[...]
            lin = jnp.dot(x, wu_ref[...].astype(BF16), preferred_element_type=F32) + bu_ref[...]
            gate = jnp.minimum(gate, SWIGLU_LIMIT)
            lin = jnp.clip(lin, -SWIGLU_LIMIT, SWIGLU_LIMIT)
            hid = (lin + 1.0) * gate * jax.nn.sigmoid(SWIGLU_ALPHA * gate)

            @pl.when(j == 0)
            def _():
                y_ref[0:m, :] = jnp.broadcast_to(bd_ref[...], (m, y_ref.shape[1]))

            y_ref[0:m, :] += jnp.dot(hid.astype(BF16), wd_ref[...].astype(BF16), preferred_element_type=F32)


def _expert_ffn(xs, tables, layer, w_gu, b_gu, w_down, b_down, n_super_max, tf=512):
    e_of, blk0, nsub, nsub_active, g_out = (tables[k] for k in ("e_of", "blk0", "nsub", "nsub_active", "g_out"))
    n_rows, dp = xs.shape
    d = 2 * dp
    n_we = w_gu.shape[0] * N_EXPERTS
    e_of = e_of + layer * N_EXPERTS
    w_gu = w_gu.reshape(n_we, d, 2 * D_FF)
    w_down = w_down.reshape(n_we, D_FF, d)
    b_gu = b_gu.reshape(n_we, 1, 2 * D_FF)
    b_down = b_down.reshape(n_we, 1, d)
    n_ff = D_FF // tf
    last = n_ff - 1

    def x_spec(s):
        return pl.BlockSpec(
            (SUB, dp), lambda g, j, e, b0, ns, na, go: (b0[g] + jnp.minimum(s, jnp.maximum(ns[g], 1) - 1), 0))

    def jj(j, na, g):
        return jnp.where(na[g] > 0, j, last)

    in_specs = [x_spec(s) for s in range(SUBS_PER_SUPER)] + [
        pl.BlockSpec((None, d, tf), lambda g, j, e, b0, ns, na, go: (e[g], 0, jj(j, na, g))),
        pl.BlockSpec((None, d, tf), lambda g, j, e, b0, ns, na, go: (e[g], 0, n_ff + jj(j, na, g))),
        pl.BlockSpec((None, 1, tf), lambda g, j, e, b0, ns, na, go: (e[g], 0, jj(j, na, g))),
        pl.BlockSpec((None, 1, tf), lambda g, j, e, b0, ns, na, go: (e[g], 0, n_ff + jj(j, na, g))),
        pl.BlockSpec((None, tf, d), lambda g, j, e, b0, ns, na, go: (e[g], jj(j, na, g), 0)),
        pl.BlockSpec((None, 1, d), lambda g, j, e, b0, ns, na, go: (e[g], 0, 0)),
    ]
    grid_spec = pltpu.PrefetchScalarGridSpec(
        num_scalar_prefetch=5,
        grid=(n_super_max, n_ff),
        in_specs=in_specs,
        out_specs=pl.BlockSpec((SUPER, d), lambda g, j, e, b0, ns, na, go: (go[g], 0)),
        scratch_shapes=[pltpu.VMEM((SUPER, d), BF16)],
    )
    return pl.pallas_call(
        functools.partial(_ffn_kernel, tf=tf),
        grid_spec=grid_spec,
        out_shape=jax.ShapeDtypeStruct((n_super_max * SUPER, d), F32),
        compiler_params=_cparams(2, FFN_VMEM_LIMIT),
        name="moe_ffn",
    )(e_of, blk0, nsub, nsub_active, g_out, *([xs] * SUBS_PER_SUPER), w_gu, w_gu, b_gu, b_gu, w_down, b_down)


def _combine_kernel(row_ref, next_ref, w_ref, h_ref, g_ref, b_ref, ys_hbm, of_ref, ob_ref, op_ref, ybuf, sems,
                    *, tm, n_tiles):
    i = pl.program_id(0)
    slot = i % 2
    n_groups = tm // ROW_UNROLL

    def issue_group(rows_smem, s, gi):
        t0 = pl.multiple_of(gi * ROW_UNROLL, ROW_UNROLL)
        for u in range(ROW_UNROLL):
            for k in range(TOP_K):
                pltpu.make_async_copy(ys_hbm.at[pl.ds(rows_smem[k, t0 + u], 1)],
                                      ybuf.at[s, k, pl.ds(t0 + u, 1)], sems.at[s]).start()

    def wait_group(gi, carry):
        t0 = pl.multiple_of(gi * ROW_UNROLL, ROW_UNROLL)
        for u in range(ROW_UNROLL):
            for k in range(TOP_K):
                pltpu.make_async_copy(ys_hbm.at[pl.ds(0, 1)], ybuf.at[slot, k, pl.ds(t0 + u, 1)],
                                      sems.at[slot]).wait()
        return carry

    @pl.when(i == 0)
    def _():
        def first(gi, carry):
            issue_group(row_ref, 0, gi)
            return carry
        lax.fori_loop(0, n_groups, first, 0)

    lax.fori_loop(0, n_groups, wait_group, 0)

    @pl.when(i + 1 < n_tiles)
    def _():
        def ahead(gi, carry):
            issue_group(next_ref, 1 - slot, gi)
            return carry
        lax.fori_loop(0, n_groups, ahead, 0)

    w = w_ref[...]
    ffn = w[:, 0:1] * ybuf[slot, 0]
    for k in range(1, TOP_K):
        ffn = ffn + w[:, k:k + 1] * ybuf[slot, k]
    y = _layer_norm_rows(DEEPNORM_ALPHA * h_ref[...] + ffn, g_ref[...], b_ref[...])
    of_ref[...] = y
    ob_ref[...] = y.astype(ob_ref.dtype)
    op_ref[...] = _pack_pairs(y)


def _combine_ln(ys, ys_row, w_t, h, g, b, tm=256):
    n_tok, d = h.shape
    n_tiles = n_tok // tm
    row = lambda i: (i, 0)
    const = lambda i: (0, 0)
    return pl.pallas_call(
        functools.partial(_combine_kernel, tm=tm, n_tiles=n_tiles),
        grid=(n_tiles,),
        in_specs=[pl.BlockSpec((TOP_K, tm), lambda i: (0, i), memory_space=pltpu.SMEM),
                  pl.BlockSpec((TOP_K, tm), lambda i: (0, jnp.minimum(i + 1, n_tiles - 1)), memory_space=pltpu.SMEM),
                  pl.BlockSpec((tm, TOP_K), row), pl.BlockSpec((tm, d), row),
                  pl.BlockSpec((1, d), const), pl.BlockSpec((1, d), const), pl.BlockSpec(memory_space=pl.ANY)],
        out_specs=[pl.BlockSpec((tm, d), row), pl.BlockSpec((tm, d), row), pl.BlockSpec((tm, d // 2), row)],
        out_shape=[jax.ShapeDtypeStruct((n_tok, d), F32), jax.ShapeDtypeStruct((n_tok, d), BF16),
                   jax.ShapeDtypeStruct((n_tok, d // 2), jnp.uint32)],
        scratch_shapes=[pltpu.VMEM((2, TOP_K, tm, d), F32), pltpu.SemaphoreType.DMA((2,))],
        compiler_params=_cparams(1),
        name="moe_combine_ln",
    )(ys_row, ys_row, w_t, h, g.reshape(1, d), b.reshape(1, d), ys)


def _moe_ln(h, h_bf, h_pk, layer, w_router, b_router, w_gu, b_gu, w_down, b_down, ln_g, ln_b):
    n_tok, d = h.shape
    n_slots = n_tok * TOP_K
    top_idx, top_w, rank, cnt = _router(h_bf, w_router, b_router)
    counts = cnt[:, 0].astype(jnp.int32)
    tables = _moe_tables(counts, n_slots)
    xs_row, ys_row = _slot_rows(top_idx, rank, tables)
    n_super_max = N_EXPERTS + n_slots // SUPER
    n_rows = n_slots + N_EXPERTS * SUB
    xs = _dispatch(h_pk, xs_row, tables, counts, n_rows)
    ys = _expert_ffn(xs, tables, layer, w_gu, b_gu, w_down, b_down, n_super_max)
    return _combine_ln(ys, ys_row, top_w.T, h, ln_g, ln_b)


def _dilated_mix(x3, w_qkv, tabs):
    b, s, d = x3.shape
    n_tok = b * s
    width = A_HEADS * HEAD_DIM
    outs, lses = [], []
    for g, (window, dil) in enumerate(A_PATTERNS):
        assert window // dil == 128
        L = s // dil
        if dil == 1:
            xp = x3.reshape(n_tok, d)
        else:
            xp = x3.reshape(b, L, dil, d).transpose(0, 2, 1, 3).reshape(n_tok, d).astype(BF16)
        tabs_p = tabs.reshape(3, L, dil, HEAD_DIM).transpose(0, 2, 1, 3).reshape(3, s, HEAD_DIM)
        qkv = _proj(xp, w_qkv, tabs_p, col0=g * 3 * width, n_out=3 * width, pos_period=s, rope_mod=3, rope_cnt=2)
        o, lse = _band_attn(qkv.reshape(b * dil, L, 3 * width))
        outs.append(o.reshape(b, dil, L, width).transpose(0, 2, 1, 3).reshape(n_tok, width))
        lses.append(lse.reshape(b, dil, L, A_HEADS).transpose(0, 2, 1, 3).reshape(n_tok, A_HEADS))
    return _mix(outs, lses)


def kernel(x, a_w_qkv, a_w_o, kv_w, b_w_q, b_w_o, router_w, router_b, moe_w_gate_up, moe_b_gate_up, moe_w_down,
           moe_b_down, ln1_g, ln1_b, ln2_g, ln2_b):
    b, s, d = x.shape
    n_tok = b * s
    tabs = _rope_tables(s)
    h = x.reshape(n_tok, d)
    h_bf = h.astype(BF16)
    kv = None
    for layer in range(DEPTH):
        if layer < N_A_LAYERS:
            mix = _dilated_mix(h.reshape(b, s, d), a_w_qkv[layer], tabs)
            w_o = a_w_o[layer]
        else:
            j = layer - N_A_LAYERS
            if layer == N_A_LAYERS:
                kv = _proj(h_bf, kv_w, tabs, col0=0, n_out=2 * B_HEADS * HEAD_DIM, pos_period=s, rope_mod=2,
                           rope_cnt=1)
            q = _proj(h_bf, b_w_q[j], tabs, col0=0, n_out=B_HEADS * HEAD_DIM, pos_period=s, rope_mod=1, rope_cnt=1)
            mix = _moba(q.reshape(b, s, -1), kv.reshape(b, s, -1)).reshape(n_tok, -1)
            w_o = b_w_o[j]
        h, h_bf, h_pk = _oproj_ln(mix, w_o.astype(BF16), h, ln1_g[layer], ln1_b[layer])
        h, h_bf, _ = _moe_ln(h, h_bf, h_pk, layer, router_w[layer], router_b[layer], moe_w_gate_up, moe_b_gate_up,
                             moe_w_down, moe_b_down, ln2_g[layer], ln2_b[layer])
    return h.reshape(b, s, d)
```
